```python
import jax, jax.numpy as jnp
from jax import lax
import numpy as np

D_MODEL = 1024
BATCH = 1
SEQ = 16384
DEPTH = 2

N_META = 16
N_HEADS = 16
N_KV_HEADS = 4
HEAD_DIM = 64
GQA_GROUP = N_HEADS // N_KV_HEADS
WINDOW = 128
BLOCK = 128
FRONT_PAD = (-N_META) % BLOCK
POOL_WINDOWS = (2, 4, 8, 16)
N_POOL_GROUPS = len(POOL_WINDOWS)
POOL_GROUP_DIM = D_MODEL // 8
POOL_DIM = N_POOL_GROUPS * POOL_GROUP_DIM
Q_DIM = N_HEADS * HEAD_DIM
KV_DIM = N_KV_HEADS * HEAD_DIM
IN_COLS = POOL_DIM + Q_DIM + 2 * KV_DIM + 2 * D_MODEL
SPLITS = [POOL_DIM, POOL_DIM + Q_DIM, POOL_DIM + Q_DIM + KV_DIM,
          POOL_DIM + Q_DIM + 2 * KV_DIM, POOL_DIM + Q_DIM + 2 * KV_DIM + D_MODEL]
D_FF = ((8 * D_MODEL // 3 + 127) // 128) * 128
RMS_EPS = 1e-6
NEG_INF = -1e30
FFN1_PRE, FFN1_POST, MIX_PRE, MIX_POST, FFN2_PRE, FFN2_POST = range(6)

kernel_name = "hybrid_pool_swa_gated_macaron"


def rmsnorm(x, g):
    xf = x.astype(jnp.float32)
    y = xf * lax.rsqrt(jnp.mean(xf * xf, axis=-1, keepdims=True) + RMS_EPS)
    return (y * g.astype(jnp.float32)).astype(x.dtype)


def swiglu(x, w_in, w_out):
    gate, up = jnp.split(x @ w_in, 2, axis=-1)
    return (jax.nn.silu(gate) * up) @ w_out


def alibi_slopes():
    m = 2.0 ** (-8.0 * np.arange(1, N_HEADS + 1) / N_HEADS)
    return jnp.asarray(m, dtype=jnp.float32).reshape(N_KV_HEADS, GQA_GROUP)


def pool_mixer(u, w_grp, scale, w_proj):
    B, L, _ = u.shape
    ug = u.reshape(B, L, N_POOL_GROUPS, POOL_GROUP_DIM)
    cs = jnp.cumsum(ug.astype(jnp.float32), axis=1)
    cs = jnp.pad(cs, ((0, 0), (1, 0), (0, 0), (0, 0)))
    t1 = jnp.arange(1, L + 1, dtype=jnp.int32)[:, None]
    win = jnp.asarray(POOL_WINDOWS, dtype=jnp.int32)[None, :]
    lag = jnp.maximum(t1 - win, 0)
    lagged = cs[:, lag, jnp.arange(N_POOL_GROUPS)[None, :], :]
    count = jnp.minimum(t1, win).astype(jnp.float32)
    pooled = (cs[:, 1:] - lagged) / count[None, :, :, None]
    mixed = (pooled - ug.astype(jnp.float32)).astype(u.dtype)
    mixed = jnp.einsum('blgc,gcd->blgd', mixed, w_grp).reshape(B, L, POOL_DIM) * scale
    return mixed @ w_proj


def swa_attention(q, k, v, sinks):
    B, L = q.shape[0], q.shape[1]
    Lp = L + FRONT_PAD
    nb = Lp // BLOCK
    padf = lambda a: jnp.pad(a, ((0, 0), (FRONT_PAD, 0), (0, 0), (0, 0)))
    qb = padf(q).reshape(B, nb, BLOCK, N_KV_HEADS, GQA_GROUP, HEAD_DIM)
    kb = padf(k).reshape(B, nb, BLOCK, N_KV_HEADS, HEAD_DIM)
    vb = padf(v).reshape(B, nb, BLOCK, N_KV_HEADS, HEAD_DIM)
    shift = lambda a: jnp.concatenate([jnp.zeros_like(a[:, :1]), a[:, :-1]], axis=1)
    kk = jnp.concatenate([shift(kb), kb], axis=2)
    vv = jnp.concatenate([shift(vb), vb], axis=2)
    k_meta, v_meta = k[:, :N_META], v[:, :N_META]
    scale = HEAD_DIM ** -0.5
    s_band = jnp.einsum('bnqkgd,bnskd->bnkgqs', qb, kk,
                        preferred_element_type=jnp.float32) * scale
    s_meta = jnp.einsum('bnqkgd,bmkd->bnkgqm', qb, k_meta,
                        preferred_element_type=jnp.float32) * scale
    q_pos = jnp.arange(Lp, dtype=jnp.int32).reshape(nb, BLOCK)
    k_pos = (jnp.arange(nb, dtype=jnp.int32)[:, None] - 1) * BLOCK + jnp.arange(2 * BLOCK, dtype=jnp.int32)[None, :]
    dist = q_pos[:, :, None] - k_pos[:, None, :]
    band_ok = (dist >= 0) & (dist < WINDOW) & (k_pos[:, None, :] >= FRONT_PAD)
    alibi_d = jnp.where(k_pos[:, None, :] >= FRONT_PAD + N_META, dist, 0).astype(jnp.float32)
    slopes = alibi_slopes()
    bias_band = jnp.where(band_ok[:, None, None],
                          -slopes[None, :, :, None, None] * alibi_d[:, None, None],
                          NEG_INF)
    s_band = s_band + bias_band[None]
    meta_pos = FRONT_PAD + jnp.arange(N_META, dtype=jnp.int32)
    meta_ok = (q_pos[:, :, None] - meta_pos[None, None, :]) >= WINDOW
    s_meta = jnp.where(meta_ok[None, :, None, None], s_meta, NEG_INF)
    sink = sinks.astype(jnp.float32).reshape(N_KV_HEADS, GQA_GROUP)[None, None, :, :, None, None]
    sink = jnp.broadcast_to(sink, s_band.shape[:-1] + (1,))
    logits = jnp.concatenate([s_band, s_meta, sink], axis=-1)
    probs = jax.nn.softmax(logits, axis=-1).astype(v.dtype)
    p_band = probs[..., :2 * BLOCK]
    p_meta = probs[..., 2 * BLOCK:2 * BLOCK + N_META]
    out = (jnp.einsum('bnkgqs,bnskd->bnqkgd', p_band, vv)
           + jnp.einsum('bnkgqm,bmkd->bnqkgd', p_meta, v_meta))
    return out.reshape(B, Lp, Q_DIM)[:, FRONT_PAD:]


def setup_inputs(seed: int = 0) -> dict:
    key = jax.random.key(seed)
    ks = jax.random.split(key, 13)
    nrm = lambda k, shape, fan_in: jax.random.normal(k, shape, jnp.float32) * (fan_in ** -0.5)
    x = jax.random.normal(ks[0], (BATCH, SEQ, D_MODEL), jnp.float32)
    meta_tokens = jax.random.normal(ks[1], (N_META, D_MODEL), jnp.float32)
    norm_g = 1.0 + 0.05 * jax.random.normal(ks[2], (DEPTH, 6, D_MODEL), jnp.float32)
    ffn_w_in = nrm(ks[3], (DEPTH, 2, D_MODEL, 2 * D_FF), D_MODEL)
    ffn_w_out = nrm(ks[4], (DEPTH, 2, D_FF, D_MODEL), D_FF)
    w_in = nrm(ks[5], (DEPTH, D_MODEL, IN_COLS), D_MODEL)
    pool_w_grp = nrm(ks[6], (DEPTH, N_POOL_GROUPS, POOL_GROUP_DIM, POOL_GROUP_DIM), POOL_GROUP_DIM)
    pool_scale = 1.0 + 0.05 * jax.random.normal(ks[7], (DEPTH, POOL_DIM), jnp.float32)
    w_pool_proj = nrm(ks[8], (DEPTH, POOL_DIM, D_MODEL), POOL_DIM)
    sinks = 0.5 * jax.random.normal(ks[9], (DEPTH, N_HEADS), jnp.float32)
    w_attn_proj = nrm(ks[10], (DEPTH, Q_DIM, D_MODEL), Q_DIM)
    w_o = nrm(ks[11], (DEPTH, D_MODEL, D_MODEL), D_MODEL)
    return {"x": x, "meta_tokens": meta_tokens, "norm_g": norm_g,
            "ffn_w_in": ffn_w_in, "ffn_w_out": ffn_w_out, "w_in": w_in,
            "pool_w_grp": pool_w_grp, "pool_scale": pool_scale,
            "w_pool_proj": w_pool_proj, "sinks": sinks,
            "w_attn_proj": w_attn_proj, "w_o": w_o}


def reference(x, meta_tokens, norm_g, ffn_w_in, ffn_w_out, w_in, pool_w_grp,
              pool_scale, w_pool_proj, sinks, w_attn_proj, w_o):
    B = x.shape[0]
    meta = jnp.broadcast_to(meta_tokens.astype(x.dtype)[None], (B, N_META, D_MODEL))
    h = jnp.concatenate([meta, x], axis=1)
    L = h.shape[1]
    for l in range(DEPTH):
        g = norm_g[l]
        h = h + 0.5 * rmsnorm(swiglu(rmsnorm(h, g[FFN1_PRE]), ffn_w_in[l, 0], ffn_w_out[l, 0]), g[FFN1_POST])
        hn = rmsnorm(h, g[MIX_PRE])
        u_pool, q, k, v, gate_pool, gate_attn = jnp.split(hn @ w_in[l], SPLITS, axis=-1)
        y_pool = pool_mixer(u_pool, pool_w_grp[l], pool_scale[l], w_pool_proj[l])
        attn = swa_attention(q.reshape(B, L, N_HEADS, HEAD_DIM),
                             k.reshape(B, L, N_KV_HEADS, HEAD_DIM),
                             v.reshape(B, L, N_KV_HEADS, HEAD_DIM), sinks[l])
        y_attn = attn @ w_attn_proj[l]
        merged = jax.nn.sigmoid(gate_pool) * y_pool + jax.nn.sigmoid(gate_attn) * y_attn
        h = h + rmsnorm(merged @ w_o[l], g[MIX_POST])
        h = h + 0.5 * rmsnorm(swiglu(rmsnorm(h, g[FFN2_PRE]), ffn_w_in[l, 1], ffn_w_out[l, 1]), g[FFN2_POST])
    return h[:, N_META:]
```

```python
import functools

import jax
import jax.numpy as jnp
from jax import lax
from jax.experimental import pallas as pl
from jax.experimental.pallas import tpu as pltpu

D_MODEL = 1024
N_META = 16
N_HEADS = 16
N_KV_HEADS = 4
HEAD_DIM = 64
GQA_GROUP = N_HEADS // N_KV_HEADS
WINDOW = 128
BLOCK = 128
FRONT_PAD = (-N_META) % BLOCK
POOL_WINDOWS = (2, 4, 8, 16)
POOL_GROUP_DIM = 128
POOL_DIM = 512
Q_DIM = 1024
KV_DIM = 256
D_FF = 2816
RMS_EPS = 1e-6
NEG_INF = -1e30
FFN1_PRE, FFN1_POST, MIX_PRE, MIX_POST, FFN2_PRE, FFN2_POST = range(6)

C_U, C_Q, C_K, C_V, C_GP, C_GA, C_END = 0, 512, 1536, 1792, 2048, 3072, 4096

POOL_HALO = 16
TILE_ROWS = 384
V7X_VMEM_BYTES = 64 * 1024 * 1024
VMEM_LIMIT = 56 * 1024 * 1024

_BF16 = jnp.bfloat16
_F32 = jnp.float32


def _rmsnorm(x, g):
    ms = jnp.mean(x * x, axis=-1, keepdims=True)
    return x * lax.rsqrt(ms + RMS_EPS) * g


def _sigmoid(x):
    return 1.0 / (1.0 + jnp.exp(-x))


def _dot(a, b):
    return jnp.dot(a, b, preferred_element_type=_F32)


def _dot_nt(a, b):
    return lax.dot_general(a, b, (((1,), (1,)), ((), ())), preferred_element_type=_F32)


def _ffn_kernel(pre, post, h_ref, g_ref, win_ref, wout_ref, o_ref):
    h = h_ref[...]
    xn = _rmsnorm(h, g_ref[pre:pre + 1, :]).astype(_BF16)
    gate = _dot(xn, win_ref[:, :D_FF])
    up = _dot(xn, win_ref[:, D_FF:])
    act = (gate * _sigmoid(gate) * up).astype(_BF16)
    y = _dot(act, wout_ref[...])
    o_ref[...] = h + 0.5 * _rmsnorm(y, g_ref[post:post + 1, :])


def _const_spec(shape):
    nd = len(shape)
    return pl.BlockSpec(shape, lambda i: (0,) * nd, pipeline_mode=pl.Buffered(1))


def _ffn_call(h, g, w_in, w_out, pre, post):
    rows = h.shape[0]
    n_tiles = rows // TILE_ROWS
    return pl.pallas_call(
        functools.partial(_ffn_kernel, pre, post),
        grid=(n_tiles,),
        in_specs=[
            pl.BlockSpec((TILE_ROWS, D_MODEL), lambda i: (i, 0)),
            _const_spec(g.shape),
            _const_spec(w_in.shape),
            _const_spec(w_out.shape),
        ],
        out_specs=pl.BlockSpec((TILE_ROWS, D_MODEL), lambda i: (i, 0)),
        out_shape=jax.ShapeDtypeStruct(h.shape, h.dtype),
        compiler_params=pltpu.CompilerParams(
            dimension_semantics=("arbitrary",), vmem_limit_bytes=VMEM_LIMIT),
        name="ffn_half_step",
    )(h, g, w_in, w_out)


def _alibi_slope(head):
    return 2.0 ** (-8.0 * (head + 1) / N_HEADS)


def _mixer_kernel(h_ref, g_ref, win_ref, wgrp_ref, pscale_ref, wpp_ref, sinks_ref, wap_ref, wo_ref,
                  o_ref, ucarry_ref, kprev_ref, vprev_ref, kmeta_ref, vmeta_ref):
    i = pl.program_id(0)
    tm = TILE_ROWS
    h = h_ref[...]
    hn = _rmsnorm(h, g_ref[MIX_PRE:MIX_PRE + 1, :]).astype(_BF16)
    row_pos = i * tm + lax.broadcasted_iota(jnp.int32, (tm, 1), 0)

    rd = lax.rem(i, 2)
    wr = 1 - rd

    @pl.when(i == 0)
    def _():
        ucarry_ref[0] = jnp.zeros((POOL_HALO, POOL_DIM), _F32)
        kprev_ref[0] = jnp.zeros((BLOCK, KV_DIM), _BF16)
        vprev_ref[0] = jnp.zeros((BLOCK, KV_DIM), _BF16)

    u = _dot(hn, win_ref[:, C_U:C_Q])
    u = jnp.where(row_pos >= FRONT_PAD, u, 0.0)
    e = jnp.concatenate([ucarry_ref[rd], u], axis=0)
    ucarry_ref[wr] = u[tm - POOL_HALO:, :]
    s2 = e + pltpu.roll(e, 1, 0)
    s4 = s2 + pltpu.roll(s2, 2, 0)
    s8 = s4 + pltpu.roll(s4, 4, 0)
    s16 = s8 + pltpu.roll(s8, 8, 0)
    tok1 = jnp.maximum(row_pos - (FRONT_PAD - 1), 1)
    mixed = []
    for gi, (win, s) in enumerate(zip(POOL_WINDOWS, (s2, s4, s8, s16))):
        lo = gi * POOL_GROUP_DIM
        cnt = jnp.minimum(tok1, win).astype(_F32)
        pooled = s[POOL_HALO:, lo:lo + POOL_GROUP_DIM] / cnt
        mx = (pooled - u[:, lo:lo + POOL_GROUP_DIM]).astype(_BF16)
        mixed.append(_dot(mx, wgrp_ref[gi]))
    mixed = (jnp.concatenate(mixed, axis=-1) * pscale_ref[...]).astype(_BF16)
    y_pool = _dot(mixed, wpp_ref[...])

    q = (_dot(hn, win_ref[:, C_Q:C_K]) * (HEAD_DIM ** -0.5)).astype(_BF16)
    k = _dot(hn, win_ref[:, C_K:C_V]).astype(_BF16)
    v = _dot(hn, win_ref[:, C_V:C_GP]).astype(_BF16)

    @pl.when(i == 0)
    def _():
        kmeta_ref[...] = k[FRONT_PAD:BLOCK, :]
        vmeta_ref[...] = v[FRONT_PAD:BLOCK, :]

    k_meta = kmeta_ref[...]
    v_meta = vmeta_ref[...]
    k_carry = kprev_ref[rd]
    v_carry = vprev_ref[rd]
    kprev_ref[wr] = k[tm - BLOCK:, :]
    vprev_ref[wr] = v[tm - BLOCK:, :]
    n_keys = 2 * BLOCK + N_META
    lane = lax.broadcasted_iota(jnp.int32, (1, BLOCK), 1)
    lo_half = lane < HEAD_DIM
    col = lax.broadcasted_iota(jnp.int32, (1, n_keys), 1)
    is_band = col < 2 * BLOCK

    attn_blocks = []
    for b in range(tm // BLOCK):
        r0 = b * BLOCK
        k_cur, v_cur = k[r0:r0 + BLOCK, :], v[r0:r0 + BLOCK, :]
        if b == 0:
            k_prv, v_prv = k_carry, v_carry
        else:
            k_prv, v_prv = k[r0 - BLOCK:r0, :], v[r0 - BLOCK:r0, :]
        kk = jnp.concatenate([k_prv, k_cur, k_meta], axis=0)
        vv = jnp.concatenate([v_prv, v_cur, v_meta], axis=0)

        q_pos = row_pos[r0:r0 + BLOCK, :]
        blk0 = i * tm + r0
        k_pos = blk0 - BLOCK + col
        dist = q_pos - k_pos
        band_ok = (dist >= 0) & (dist < WINDOW) & (k_pos >= FRONT_PAD)
        meta_ok = (q_pos - (FRONT_PAD + col - 2 * BLOCK)) >= WINDOW
        ok = (is_band & band_ok) | (jnp.logical_not(is_band) & meta_ok)
        alibi_d = jnp.where(is_band & (k_pos >= FRONT_PAD + N_META), dist, 0).astype(_F32)

        tiles = []
        for t in range(Q_DIM // 128):
            kh = (2 * t) // GQA_GROUP
            q_pair = q[r0:r0 + BLOCK, t * 128:(t + 1) * 128]
            kk_t = kk[:, (kh // 2) * 128:(kh // 2 + 1) * 128]
            vv_t = vv[:, (kh // 2) * 128:(kh // 2 + 1) * 128]
            if kh % 2 == 1:
                kk_t = jnp.concatenate([kk_t[:, HEAD_DIM:], kk_t[:, :HEAD_DIM]], axis=-1)
                vv_t = jnp.concatenate([vv_t[:, HEAD_DIM:], vv_t[:, :HEAD_DIM]], axis=-1)
            zero = jnp.zeros_like(kk_t)
            k_lo = jnp.where(lo_half, kk_t, zero)
            v_lo = jnp.where(lo_half, vv_t, zero)
            k_hi = jnp.concatenate([zero[:, :HEAD_DIM], kk_t[:, :HEAD_DIM]], axis=-1)
            v_hi = jnp.concatenate([zero[:, :HEAD_DIM], vv_t[:, :HEAD_DIM]], axis=-1)
            out_pair = None
            for j, (k_z, v_z) in enumerate(((k_lo, v_lo), (k_hi, v_hi))):
                head = 2 * t + j
                s = _dot_nt(q_pair, k_z)
                logits = jnp.where(ok, s - _alibi_slope(head) * alibi_d, NEG_INF)
                sink = sinks_ref[head]
                m = jnp.maximum(jnp.max(logits, axis=-1, keepdims=True), sink)
                p = jnp.exp(logits - m)
                denom = jnp.sum(p, axis=-1, keepdims=True) + jnp.exp(sink - m)
                probs = (p / denom).astype(_BF16)
                o = _dot(probs, v_z)
                out_pair = o if out_pair is None else out_pair + o
            tiles.append(out_pair)
        attn_blocks.append(jnp.concatenate(tiles, axis=-1))

    attn =jnp.concatenate(attn_blocks, axis=0).astype(_BF16)
    y_attn = _dot(attn, wap_ref[...])

    gate_pool = _dot(hn, win_ref[:, C_GP:C_GA])
    gate_attn = _dot(hn, win_ref[:, C_GA:C_END])
    merged = (_sigmoid(gate_pool) * y_pool + _sigmoid(gate_attn) * y_attn).astype(_BF16)
    z = _dot(merged, wo_ref[...])
    o_ref[...] = h + _rmsnorm(z, g_ref[MIX_POST:MIX_POST + 1, :])


def _mixer_call(h, g, w_in, w_grp, pscale, w_pp, sinks, w_ap, w_o):
    rows = h.shape[0]
    n_tiles = rows // TILE_ROWS
    return pl.pallas_call(
        _mixer_kernel,
        grid=(n_tiles,),
        in_specs=[
            pl.BlockSpec((TILE_ROWS, D_MODEL), lambda i: (i, 0)),
            _const_spec(g.shape),
            _const_spec(w_in.shape),
            _const_spec(w_grp.shape),
            _const_spec(pscale.shape),
            _const_spec(w_pp.shape),
            pl.BlockSpec(memory_space=pltpu.SMEM),
            _const_spec(w_ap.shape),
            _const_spec(w_o.shape),
        ],
        out_specs=pl.BlockSpec((TILE_ROWS, D_MODEL), lambda i: (i, 0)),
        out_shape=jax.ShapeDtypeStruct(h.shape, h.dtype),
        scratch_shapes=[
            pltpu.VMEM((2, POOL_HALO, POOL_DIM), _F32),
            pltpu.VMEM((2, BLOCK, KV_DIM), _BF16),
            pltpu.VMEM((2, BLOCK, KV_DIM), _BF16),
            pltpu.VMEM((N_META, KV_DIM), _BF16),
            pltpu.VMEM((N_META, KV_DIM), _BF16),
        ],
        compiler_params=pltpu.CompilerParams(
            dimension_semantics=("arbitrary",), vmem_limit_bytes=VMEM_LIMIT),
        name="gated_mixers",
    )(h, g, w_in, w_grp, pscale, w_pp, sinks, w_ap, w_o)


def kernel(x, meta_tokens, norm_g, ffn_w_in, ffn_w_out, w_in, pool_w_grp, pool_scale, w_pool_proj, sinks,
           w_attn_proj, w_o):
    batch, seq, d = x.shape
    assert batch == 1 and d == D_MODEL and (FRONT_PAD + N_META + seq) % TILE_ROWS == 0
    depth = norm_g.shape[0]
    h = jnp.concatenate(
        [jnp.zeros((FRONT_PAD, d), x.dtype), meta_tokens.astype(x.dtype), x[0]], axis=0)
    for l in range(depth):
        g = norm_g[l]
        h = _ffn_call(h, g, ffn_w_in[l, 0].astype(_BF16), ffn_w_out[l, 0].astype(_BF16), FFN1_PRE, FFN1_POST)
        h = _mixer_call(h, g, w_in[l].astype(_BF16), pool_w_grp[l].astype(_BF16),
                        pool_scale[l].reshape(1, POOL_DIM), w_pool_proj[l].astype(_BF16), sinks[l],
                        w_attn_proj[l].astype(_BF16), w_o[l].astype(_BF16))
        h = _ffn_call(h, g, ffn_w_in[l, 1].astype(_BF16), ffn_w_out[l, 1].astype(_BF16), FFN2_PRE, FFN2_POST)
    return h[FRONT_PAD + N_META:][None]
```

```python
import functools

import jax
import jax.numpy as jnp
from jax import lax
from jax.experimental import pallas as pl
from jax.experimental.pallas import tpu as pltpu

D_MODEL = 1024
N_META = 16
N_HEADS = 16
N_KV_HEADS = 4
HEAD_DIM = 64
GQA_GROUP = N_HEADS // N_KV_HEADS
WINDOW = 128
BLOCK = 128
FRONT_PAD = (-N_META) % BLOCK
POOL_WINDOWS = (2, 4, 8, 16)
POOL_GROUP_DIM = 128
POOL_DIM = 512
Q_DIM = 1024
KV_DIM = 256
D_FF = 2816
RMS_EPS = 1e-6
NEG_INF = -1e30
FFN1_PRE, FFN1_POST, MIX_PRE, MIX_POST, FFN2_PRE, FFN2_POST = range(6)

C_U, C_Q, C_K, C_V, C_GP, C_GA, C_END = 0, 512, 1536, 1792, 2048, 3072, 4096

POOL_HALO = 16
TILE_ROWS = 384
V7X_VMEM_BYTES = 64 * 1024 * 1024
VMEM_LIMIT = 56 * 1024 * 1024

_BF16 = jnp.bfloat16
_F32 = jnp.float32


def _rmsnorm(x, g):
    ms = jnp.mean(x * x, axis=-1, keepdims=True)
    return x * lax.rsqrt(ms + RMS_EPS) * g


def _sigmoid(x):
    return 1.0 / (1.0 + jnp.exp(-x))


def _dot(a, b):
    return jnp.dot(a, b, preferred_element_type=_F32)


def _dot_nt(a, b):
    return lax.dot_general(a, b, (((1,), (1,)), ((), ())), preferred_element_type=_F32)


def _dot_tn(a, b):
    return lax.dot_general(a, b, (((0,), (0,)), ((), ())), preferred_element_type=_F32)


def _ffn_kernel(pre, post, h_ref, g_ref, win_ref, wout_ref, o_ref):
    h = h_ref[...]
    xn = _rmsnorm(h, g_ref[pre:pre + 1, :]).astype(_BF16)
    gate = _dot(xn, win_ref[:, :D_FF])
    up = _dot(xn, win_ref[:, D_FF:])
    act = (gate * _sigmoid(gate) * up).astype(_BF16)
    y = _dot(act, wout_ref[...])
    o_ref[...] = h + 0.5 * _rmsnorm(y, g_ref[post:post + 1, :])


def _const_spec(shape):
    nd = len(shape)
    return pl.BlockSpec(shape, lambda i: (0,) * nd, pipeline_mode=pl.Buffered(1))


def _ffn_call(h, g, w_in, w_out, pre, post):
    rows = h.shape[0]
    n_tiles = rows // TILE_ROWS
    return pl.pallas_call(
        functools.partial(_ffn_kernel, pre, post),
        grid=(n_tiles,),
        in_specs=[
            pl.BlockSpec((TILE_ROWS, D_MODEL), lambda i: (i, 0)),
            _const_spec(g.shape),
            _const_spec(w_in.shape),
            _const_spec(w_out.shape),
        ],
        out_specs=pl.BlockSpec((TILE_ROWS, D_MODEL), lambda i: (i, 0)),
        out_shape=jax.ShapeDtypeStruct(h.shape, h.dtype),
        compiler_params=pltpu.CompilerParams(
            dimension_semantics=("arbitrary",), vmem_limit_bytes=VMEM_LIMIT),
        name="ffn_half_step",
    )(h, g, w_in, w_out)


def _alibi_slope(head):
    return 2.0 ** (-8.0 * (head + 1) / N_HEADS)


def _mixer_kernel(h_ref, g_ref, win_ref, wqt_ref, wvt_ref, wgrp_ref, pscale_ref, wpp_ref, sinks_ref, wap_ref, wo_ref,
                  o_ref, ucarry_ref, kprev_ref, vprev_ref, kmeta_ref, vmeta_ref):
    i = pl.program_id(0)
    tm = TILE_ROWS
    h = h_ref[...]
    hn = _rmsnorm(h, g_ref[MIX_PRE:MIX_PRE + 1, :]).astype(_BF16)
    row_pos = i * tm + lax.broadcasted_iota(jnp.int32, (tm, 1), 0)

    rd = lax.rem(i, 2)
    wr = 1 - rd

    @pl.when(i == 0)
    def _():
        ucarry_ref[0] = jnp.zeros((POOL_HALO, POOL_DIM), _F32)
        kprev_ref[0] = jnp.zeros((BLOCK, KV_DIM), _BF16)
        vprev_ref[0] = jnp.zeros((KV_DIM, BLOCK), _BF16)

    u = _dot(hn, win_ref[:, C_U:C_Q])
    u = jnp.where(row_pos >= FRONT_PAD, u, 0.0)
    e = jnp.concatenate([ucarry_ref[rd], u], axis=0)
    ucarry_ref[wr] = u[tm - POOL_HALO:, :]
    s2 = e + pltpu.roll(e, 1, 0)
    s4 = s2 + pltpu.roll(s2, 2, 0)
    s8 = s4 + pltpu.roll(s4, 4, 0)
    s16 = s8 + pltpu.roll(s8, 8, 0)
    tok1 = jnp.maximum(row_pos - (FRONT_PAD - 1), 1)
    mixed = []
    for gi, (win, s) in enumerate(zip(POOL_WINDOWS, (s2, s4, s8, s16))):
        lo = gi * POOL_GROUP_DIM
        cnt = jnp.minimum(tok1, win).astype(_F32)
        pooled = s[POOL_HALO:, lo:lo + POOL_GROUP_DIM] / cnt
        mx = (pooled - u[:, lo:lo + POOL_GROUP_DIM]).astype(_BF16)
        mixed.append(_dot(mx, wgrp_ref[gi]))
    mixed = (jnp.concatenate(mixed, axis=-1) * pscale_ref[...]).astype(_BF16)
    y_pool = _dot(mixed, wpp_ref[...])

    q_t = (_dot_nt(wqt_ref[...], hn) * (HEAD_DIM ** -0.5)).astype(_BF16)
    v_t = _dot_nt(wvt_ref[...], hn).astype(_BF16)
    k = _dot(hn, win_ref[:, C_K:C_V]).astype(_BF16)

    @pl.when(i == 0)
    def _():
        kmeta_ref[...] = k[FRONT_PAD:BLOCK, :]
        vmeta_ref[...] = v_t[:, FRONT_PAD:BLOCK]

    k_meta = kmeta_ref[...]
    v_meta_t = vmeta_ref[...]
    k_carry = kprev_ref[rd]
    v_carry_t = vprev_ref[rd]
    kprev_ref[wr] = k[tm - BLOCK:, :]
    vprev_ref[wr] = v_t[:, tm - BLOCK:]
    n_keys = 2 * BLOCK + N_META
    key_idx = lax.broadcasted_iota(jnp.int32, (n_keys, 1), 0)
    is_band = key_idx < 2 * BLOCK
    zeros_q = jnp.zeros((HEAD_DIM, GQA_GROUP * BLOCK), _BF16)

    attn_blocks = []
    for b in range(tm // BLOCK):
        r0 = b * BLOCK
        k_prv = k_carry if b == 0 else k[r0 - BLOCK:r0, :]
        v_prv_t = v_carry_t if b == 0 else v_t[:, r0 - BLOCK:r0]
        kk = jnp.concatenate([k_prv, k[r0:r0 + BLOCK, :], k_meta], axis=0)
        vv_t = jnp.concatenate([v_prv_t, v_t[:, r0:r0 + BLOCK], v_meta_t], axis=1)

        blk0 = i * tm + r0
        q_pos = blk0 + lax.broadcasted_iota(jnp.int32, (1, BLOCK), 1)
        k_pos = blk0 - BLOCK + key_idx
        dist = q_pos - k_pos
        band_ok = (dist >= 0) & (dist < WINDOW) & (k_pos >= FRONT_PAD)
        meta_ok = (q_pos - (FRONT_PAD + key_idx - 2 * BLOCK)) >= WINDOW
        ok = (is_band & band_ok) | (jnp.logical_not(is_band) & meta_ok)
        alibi_d = jnp.where(is_band & (k_pos >= FRONT_PAD + N_META), dist, 0).astype(_F32)

        head_rows = []
        for kh in range(N_KV_HEADS):
            q_stack = jnp.concatenate(
                [q_t[(kh * GQA_GROUP + g) * HEAD_DIM:(kh * GQA_GROUP + g + 1) * HEAD_DIM, r0:r0 + BLOCK]
                 for g in range(GQA_GROUP)], axis=1)
            q_z = jnp.concatenate([q_stack, zeros_q] if kh % 2 == 0 else [zeros_q, q_stack], axis=0)
            kk_tile = kk[:, (kh // 2) * 128:(kh // 2 + 1) * 128]
            s_t = _dot(kk_tile, q_z)
            probs, inv_l = [], []
            for g in range(GQA_GROUP):
                head = kh * GQA_GROUP + g
                s_h = s_t[:, g * BLOCK:(g + 1) * BLOCK]
                logits = jnp.where(ok, s_h - _alibi_slope(head) * alibi_d, NEG_INF)
                sink = sinks_ref[head]
                m = jnp.maximum(jnp.max(logits, axis=0, keepdims=True), sink)
                p = jnp.exp(logits - m)
                denom = jnp.sum(p, axis=0, keepdims=True) + jnp.exp(sink - m)
                probs.append(p.astype(_BF16))
                inv_l.append(1.0 / denom)
            p_t = jnp.concatenate(probs, axis=1)
            o_t = _dot(vv_t[kh * HEAD_DIM:(kh + 1) * HEAD_DIM, :], p_t)
            for g in range(GQA_GROUP):
                head_rows.append(o_t[:, g * BLOCK:(g + 1) * BLOCK] * inv_l[g])
        attn_blocks.append(jnp.concatenate(head_rows, axis=0))

    attn_t = jnp.concatenate(attn_blocks, axis=1).astype(_BF16)
    y_attn = _dot_tn(attn_t, wap_ref[...])

    gate_pool = _dot(hn, win_ref[:, C_GP:C_GA])
    gate_attn = _dot(hn, win_ref[:, C_GA:C_END])
    merged = (_sigmoid(gate_pool) * y_pool + _sigmoid(gate_attn) * y_attn).astype(_BF16)
    z = _dot(merged, wo_ref[...])
    o_ref[...] = h + _rmsnorm(z, g_ref[MIX_POST:MIX_POST + 1, :])


def _mixer_call(h, g, w_in, w_qt, w_vt, w_grp, pscale, w_pp, sinks, w_ap, w_o):
    rows = h.shape[0]
    n_tiles = rows // TILE_ROWS
    return pl.pallas_call(
        _mixer_kernel,
        grid=(n_tiles,),
        in_specs=[
            pl.BlockSpec((TILE_ROWS, D_MODEL), lambda i: (i, 0)),
            _const_spec(g.shape),
            _const_spec(w_in.shape),
            _const_spec(w_qt.shape),
            _const_spec(w_vt.shape),
            _const_spec(w_grp.shape),
            _const_spec(pscale.shape),
            _const_spec(w_pp.shape),
            pl.BlockSpec(memory_space=pltpu.SMEM),
            _const_spec(w_ap.shape),
            _const_spec(w_o.shape),
        ],
        out_specs=pl.BlockSpec((TILE_ROWS, D_MODEL), lambda i: (i, 0)),
        out_shape=jax.ShapeDtypeStruct(h.shape, h.dtype),
        scratch_shapes=[
            pltpu.VMEM((2, POOL_HALO, POOL_DIM), _F32),
            pltpu.VMEM((2, BLOCK, KV_DIM), _BF16),
            pltpu.VMEM((2, KV_DIM, BLOCK), _BF16),
            pltpu.VMEM((N_META, KV_DIM), _BF16),
            pltpu.VMEM((KV_DIM, N_META), _BF16),
        ],
        compiler_params=pltpu.CompilerParams(
            dimension_semantics=("arbitrary",), vmem_limit_bytes=VMEM_LIMIT),
        name="gated_mixers",
    )(h, g, w_in, w_qt, w_vt, w_grp, pscale, w_pp, sinks, w_ap, w_o)


def kernel(x, meta_tokens, norm_g, ffn_w_in, ffn_w_out, w_in, pool_w_grp, pool_scale, w_pool_proj, sinks,
           w_attn_proj, w_o):
    batch, seq, d = x.shape
    assert batch == 1 and d == D_MODEL and (FRONT_PAD + N_META + seq) % TILE_ROWS == 0
    depth = norm_g.shape[0]
    h = jnp.concatenate(
        [jnp.zeros((FRONT_PAD, d), x.dtype), meta_tokens.astype(x.dtype), x[0]], axis=0)
    for l in range(depth):
        g = norm_g[l]
        h = _ffn_call(h, g, ffn_w_in[l, 0].astype(_BF16), ffn_w_out[l, 0].astype(_BF16), FFN1_PRE, FFN1_POST)
        w_in_l = w_in[l].astype(_BF16)
        h = _mixer_call(h, g, w_in_l, w_in_l[:, C_Q:C_K].T, w_in_l[:, C_V:C_GP].T, pool_w_grp[l].astype(_BF16),
                        pool_scale[l].reshape(1, POOL_DIM), w_pool_proj[l].astype(_BF16), sinks[l],
                        w_attn_proj[l].astype(_BF16), w_o[l].astype(_BF16))
        h = _ffn_call(h, g, ffn_w_in[l, 1].astype(_BF16), ffn_w_out[l, 1].astype(_BF16), FFN2_PRE, FFN2_POST)
    return h[FRONT_PAD + N_META:][None]
```

```python
import functools

import jax
import jax.numpy as jnp
from jax import lax
from jax.experimental import pallas as pl
from jax.experimental.pallas import tpu as pltpu

D_MODEL = 1024
N_META = 16
N_HEADS = 16
N_KV_HEADS = 4
HEAD_DIM = 64
GQA_GROUP = N_HEADS // N_KV_HEADS
WINDOW = 128
BLOCK = 128
FRONT_PAD = (-N_META) % BLOCK
POOL_WINDOWS = (2, 4, 8, 16)
POOL_GROUP_DIM = 128
POOL_DIM = 512
Q_DIM = 1024
KV_DIM = 256
D_FF = 2816
RMS_EPS = 1e-6
NEG_INF = -1e30
FFN1_PRE, FFN1_POST, MIX_PRE, MIX_POST, FFN2_PRE, FFN2_POST = range(6)

C_U, C_Q, C_K, C_V, C_GP, C_GA, C_END = 0, 512, 1536, 1792, 2048, 3072, 4096

GATE_CHUNK = 256
POOL_HALO = 16
TILE_ROWS = 384
V7X_VMEM_BYTES = 64 * 1024 * 1024
VMEM_LIMIT = 56 * 1024 * 1024

_BF16 = jnp.bfloat16
_F32 = jnp.float32


def _rmsnorm(x, g):
    ms = jnp.mean(x * x, axis=-1, keepdims=True)
    return x * lax.rsqrt(ms + RMS_EPS) * g


def _sigmoid(x):
    return 1.0 / (1.0 + jnp.exp(-x))


def _dot(a, b):
    return jnp.dot(a, b, preferred_element_type=_F32)


def _dot_nt(a, b):
    return lax.dot_general(a, b, (((1,), (1,)), ((), ())), preferred_element_type=_F32)


def _dot_tn(a, b):
    return lax.dot_general(a, b, (((0,), (0,)), ((), ())), preferred_element_type=_F32)


def _ffn_kernel(pre, post, h_ref, g_ref, win_ref, wout_ref, o_ref):
    h = h_ref[...]
    xn = _rmsnorm(h, g_ref[pre:pre + 1, :]).astype(_BF16)
    gate = _dot(xn, win_ref[:, :D_FF])
    up = _dot(xn, win_ref[:, D_FF:])
    act = (gate * _sigmoid(gate) * up).astype(_BF16)
    y = _dot(act, wout_ref[...])
    o_ref[...] = h + 0.5 * _rmsnorm(y, g_ref[post:post + 1, :])


def _const_spec(shape):
    nd = len(shape)
    return pl.BlockSpec(shape, lambda i: (0,) * nd, pipeline_mode=pl.Buffered(1))


def _ffn_call(h, g, w_in, w_out, pre, post):
    rows = h.shape[0]
    n_tiles = rows // TILE_ROWS
    return pl.pallas_call(
        functools.partial(_ffn_kernel, pre, post),
        grid=(n_tiles,),
        in_specs=[
            pl.BlockSpec((TILE_ROWS, D_MODEL), lambda i: (i, 0)),
            _const_spec(g.shape),
            _const_spec(w_in.shape),
            _const_spec(w_out.shape),
        ],
        out_specs=pl.BlockSpec((TILE_ROWS, D_MODEL), lambda i: (i, 0)),
        out_shape=jax.ShapeDtypeStruct(h.shape, h.dtype),
        compiler_params=pltpu.CompilerParams(
            dimension_semantics=("arbitrary",), vmem_limit_bytes=VMEM_LIMIT),
        name="ffn_half_step",
    )(h, g, w_in, w_out)


def _alibi_slope(head):
    return 2.0 ** (-8.0 * (head + 1) / N_HEADS)


def _mixer_kernel(h_ref, g_ref, win_ref, wqt_ref, wvt_ref, wgrp_ref, pscale_ref, wpp_ref, sinks_ref, wap_ref, wo_ref,
                  o_ref, ucarry_ref, kprev_ref, vprev_ref, kmeta_ref, vmeta_ref):
    i = pl.program_id(0)
    tm = TILE_ROWS
    h = h_ref[...]
    hn = _rmsnorm(h, g_ref[MIX_PRE:MIX_PRE + 1, :]).astype(_BF16)
    row_pos = i * tm + lax.broadcasted_iota(jnp.int32, (tm, 1), 0)

    rd = lax.rem(i, 2)
    wr = 1 - rd

    @pl.when(i == 0)
    def _():
        ucarry_ref[0] = jnp.zeros((POOL_HALO, POOL_DIM), _F32)
        kprev_ref[0] = jnp.zeros((BLOCK, KV_DIM), _BF16)
        vprev_ref[0] = jnp.zeros((KV_DIM, BLOCK), _BF16)

    k = _dot(hn, win_ref[:, C_K:C_V]).astype(_BF16)
    v_t = _dot_nt(wvt_ref[...], hn).astype(_BF16)
    q_t = (_dot_nt(wqt_ref[...], hn) * (HEAD_DIM ** -0.5)).astype(_BF16)

    @pl.when(i == 0)
    def _():
        kmeta_ref[...] = k[FRONT_PAD:BLOCK, :]
        vmeta_ref[...] = v_t[:, FRONT_PAD:BLOCK]

    k_meta = kmeta_ref[...]
    v_meta_t = vmeta_ref[...]
    k_carry = kprev_ref[rd]
    v_carry_t = vprev_ref[rd]
    kprev_ref[wr] = k[tm - BLOCK:, :]
    vprev_ref[wr] = v_t[:, tm - BLOCK:]

    def pool_sums():
        u = _dot(hn, win_ref[:, C_U:C_Q])
        u = jnp.where(row_pos >= FRONT_PAD, u, 0.0)
        e = jnp.concatenate([ucarry_ref[rd], u], axis=0)
        ucarry_ref[wr] = u[tm - POOL_HALO:, :]
        s2 = e + pltpu.roll(e, 1, 0)
        s4 = s2 + pltpu.roll(s2, 2, 0)
        s8 = s4 + pltpu.roll(s4, 4, 0)
        s16 = s8 + pltpu.roll(s8, 8, 0)
        tok1 = jnp.maximum(row_pos - (FRONT_PAD - 1), 1)
        mixed = []
        for gi, (win, s) in enumerate(zip(POOL_WINDOWS, (s2, s4, s8, s16))):
            lo = gi * POOL_GROUP_DIM
            cnt = jnp.minimum(tok1, win).astype(_F32)
            pooled = s[POOL_HALO:, lo:lo + POOL_GROUP_DIM] / cnt
            mixed.append((pooled - u[:, lo:lo + POOL_GROUP_DIM]).astype(_BF16))
        return mixed

    def pool_proj(mixed):
        mixed = [_dot(mx, wgrp_ref[gi]) for gi, mx in enumerate(mixed)]
        mixed = (jnp.concatenate(mixed, axis=-1) * pscale_ref[...]).astype(_BF16)
        return _dot(mixed, wpp_ref[...])

    def gate_chunk(c):
        return _sigmoid(_dot(hn, win_ref[:, C_GP + c * GATE_CHUNK:C_GP + (c + 1) * GATE_CHUNK]))

    n_gate_chunks = (C_END - C_GP) // GATE_CHUNK
    side = {}
    side_work = [lambda: side.__setitem__("mixed", pool_sums()),
                 lambda: side.__setitem__("y_pool", pool_proj(side["mixed"]))]
    side_work += [functools.partial(lambda c: side.__setitem__(("gate", c), gate_chunk(c)), c)
                  for c in range(n_gate_chunks)]

    n_keys = 2 * BLOCK + N_META
    key_idx = lax.broadcasted_iota(jnp.int32, (n_keys, 1), 0)
    is_band = key_idx < 2 * BLOCK
    zeros_q = jnp.zeros((HEAD_DIM, GQA_GROUP * BLOCK), _BF16)

    def block_masks(b):
        blk0 = i * tm + b * BLOCK
        q_pos = blk0 + lax.broadcasted_iota(jnp.int32, (1, BLOCK), 1)
        k_pos = blk0 - BLOCK + key_idx
        dist = q_pos - k_pos
        band_ok = (dist >= 0) & (dist < WINDOW) & (k_pos >= FRONT_PAD)
        meta_ok = (q_pos - (FRONT_PAD + key_idx - 2 * BLOCK)) >= WINDOW
        ok = (is_band & band_ok) | (jnp.logical_not(is_band) & meta_ok)
        alibi_d = jnp.where(is_band & (k_pos >= FRONT_PAD + N_META), dist, 0).astype(_F32)
        return ok, alibi_d

    def scores(b, kh):
        r0 = b * BLOCK
        k_prv = k_carry if b == 0 else k[r0 - BLOCK:r0, :]
        lanes = slice((kh // 2) * 128, (kh // 2 + 1) * 128)
        kk = jnp.concatenate([k_prv[:, lanes], k[r0:r0 + BLOCK, lanes], k_meta[:, lanes]], axis=0)
        q_stack = jnp.concatenate(
            [q_t[(kh * GQA_GROUP + g) * HEAD_DIM:(kh * GQA_GROUP + g + 1) * HEAD_DIM, r0:r0 + BLOCK]
             for g in range(GQA_GROUP)], axis=1)
        q_z = jnp.concatenate([q_stack, zeros_q] if kh % 2 == 0 else [zeros_q, q_stack], axis=0)
        return _dot(kk, q_z)

    def attend(b, kh, s_t, ok, alibi_d):
        r0 = b * BLOCK
        v_prv_t = v_carry_t if b == 0 else v_t[:, r0 - BLOCK:r0]
        rows = slice(kh * HEAD_DIM, (kh + 1) * HEAD_DIM)
        vv_t = jnp.concatenate([v_prv_t[rows, :], v_t[rows, r0:r0 + BLOCK], v_meta_t[rows, :]], axis=1)
        probs, inv_l = [], []
        for g in range(GQA_GROUP):
            head = kh * GQA_GROUP + g
            s_h = s_t[:, g * BLOCK:(g + 1) * BLOCK]
            logits = jnp.where(ok, s_h - _alibi_slope(head) * alibi_d, NEG_INF)
            sink = sinks_ref[head]
            m = jnp.maximum(jnp.max(logits, axis=0, keepdims=True), sink)
            p = jnp.exp(logits - m)
            denom = jnp.sum(p, axis=0, keepdims=True) + jnp.exp(sink - m)
            probs.append(p.astype(_BF16))
            inv_l.append(1.0 / denom)
        o_t = _dot(vv_t, jnp.concatenate(probs, axis=1))
        return [o_t[:, g * BLOCK:(g + 1) * BLOCK] * inv_l[g] for g in range(GQA_GROUP)]

    steps = [(b, kh) for b in range(tm // BLOCK) for kh in range(N_KV_HEADS)]
    masks = {}
    head_rows = {b: [] for b in range(tm // BLOCK)}
    s_next = scores(*steps[0])
    for n, (b, kh) in enumerate(steps):
        s_cur = s_next
        if n + 1 < len(steps):
            s_next = scores(*steps[n + 1])
        if b not in masks:
            masks[b] = block_masks(b)
        head_rows[b] += attend(b, kh, s_cur, *masks[b])
        if n < len(side_work):
            side_work[n]()
    for work in side_work[len(steps):]:
        work()

    attn_t = jnp.concatenate(
        [jnp.concatenate(head_rows[b], axis=0) for b in range(tm // BLOCK)], axis=1).astype(_BF16)
    y_attn = _dot_tn(attn_t, wap_ref[...])

    gates = jnp.concatenate([side[("gate", c)] for c in range(n_gate_chunks)], axis=-1)
    merged = (gates[:, :D_MODEL] * side["y_pool"] + gates[:, D_MODEL:] * y_attn).astype(_BF16)
    z = _dot(merged, wo_ref[...])
    o_ref[...] = h + _rmsnorm(z, g_ref[MIX_POST:MIX_POST + 1, :])


def _mixer_call(h, g, w_in, w_qt, w_vt, w_grp, pscale, w_pp, sinks, w_ap, w_o):
    rows = h.shape[0]
    n_tiles = rows // TILE_ROWS
    return pl.pallas_call(
        _mixer_kernel,
        grid=(n_tiles,),
        in_specs=[
            pl.BlockSpec((TILE_ROWS, D_MODEL), lambda i: (i, 0)),
            _const_spec(g.shape),
            _const_spec(w_in.shape),
            _const_spec(w_qt.shape),
            _const_spec(w_vt.shape),
            _const_spec(w_grp.shape),
            _const_spec(pscale.shape),
            _const_spec(w_pp.shape),
            pl.BlockSpec(memory_space=pltpu.SMEM),
            _const_spec(w_ap.shape),
            _const_spec(w_o.shape),
        ],
        out_specs=pl.BlockSpec((TILE_ROWS, D_MODEL), lambda i: (i, 0)),
        out_shape=jax.ShapeDtypeStruct(h.shape, h.dtype),
        scratch_shapes=[
            pltpu.VMEM((2, POOL_HALO, POOL_DIM), _F32),
            pltpu.VMEM((2, BLOCK, KV_DIM), _BF16),
            pltpu.VMEM((2, KV_DIM, BLOCK), _BF16),
            pltpu.VMEM((N_META, KV_DIM), _BF16),
            pltpu.VMEM((KV_DIM, N_META), _BF16),
        ],
        compiler_params=pltpu.CompilerParams(
            dimension_semantics=("arbitrary",), vmem_limit_bytes=VMEM_LIMIT),
        name="gated_mixers",
    )(h, g, w_in, w_qt, w_vt, w_grp, pscale, w_pp, sinks, w_ap, w_o)


def kernel(x, meta_tokens, norm_g, ffn_w_in, ffn_w_out, w_in, pool_w_grp, pool_scale, w_pool_proj, sinks,
           w_attn_proj, w_o):
    batch, seq, d = x.shape
    assert batch == 1 and d == D_MODEL and (FRONT_PAD + N_META + seq) % TILE_ROWS == 0
    depth = norm_g.shape[0]
    h = jnp.concatenate(
        [jnp.zeros((FRONT_PAD, d), x.dtype), meta_tokens.astype(x.dtype), x[0]], axis=0)
    for l in range(depth):
        g = norm_g[l]
        h = _ffn_call(h, g, ffn_w_in[l, 0].astype(_BF16), ffn_w_out[l, 0].astype(_BF16), FFN1_PRE, FFN1_POST)
        w_in_l = w_in[l].astype(_BF16)
        h = _mixer_call(h, g, w_in_l, w_in_l[:, C_Q:C_K].T, w_in_l[:, C_V:C_GP].T, pool_w_grp[l].astype(_BF16),
                        pool_scale[l].reshape(1, POOL_DIM), w_pool_proj[l].astype(_BF16), sinks[l],
                        w_attn_proj[l].astype(_BF16), w_o[l].astype(_BF16))
        h = _ffn_call(h, g, ffn_w_in[l, 1].astype(_BF16), ffn_w_out[l, 1].astype(_BF16), FFN2_PRE, FFN2_POST)
    return h[FRONT_PAD + N_META:][None]
```

```python
import functools

import jax
import jax.numpy as jnp
from jax import lax
from jax.experimental import pallas as pl
from jax.experimental.pallas import tpu as pltpu

D_MODEL = 1024
N_META = 16
N_HEADS = 16
N_KV_HEADS = 4
HEAD_DIM = 64
GQA_GROUP = N_HEADS // N_KV_HEADS
WINDOW = 128
BLOCK = 128
POOL_WINDOWS = (2, 4, 8, 16)
POOL_GROUP_DIM = 128
POOL_DIM = 512
KV_DIM = 256
D_FF = 2816
RMS_EPS = 1e-6
NEG_INF = -1e30
FFN1_PRE, FFN1_POST, MIX_PRE, MIX_POST, FFN2_PRE, FFN2_POST = range(6)

C_U, C_Q, C_K, C_V, C_GP, C_GA, C_END = 0, 512, 1536, 1792, 2048, 3072, 4096

GATE_CHUNK = 256
POOL_HALO = 16
TILE_ROWS = 512
VMEM_LIMIT = 56 * 1024 * 1024
CAST_BLOCK_BYTES = 4 * 1024 * 1024

_BF16 = jnp.bfloat16
_F32 = jnp.float32


def _rmsnorm(x, g):
    ms = jnp.mean(x * x, axis=-1, keepdims=True)
    return x * lax.rsqrt(ms + RMS_EPS) * g


def _sigmoid(x):
    return 1.0 / (1.0 + jnp.exp(-x))


def _dot(a, b):
    return jnp.dot(a, b, preferred_element_type=_F32)


def _dot_tn(a, b):
    return lax.dot_general(a, b, (((0,), (0,)), ((), ())), preferred_element_type=_F32)


def _proj_t(w, x):
    return lax.dot_general(w, x, (((0,), (1,)), ((), ())), preferred_element_type=_F32)


def _const_spec(block_shape, index):
    return pl.BlockSpec(block_shape, lambda i: index, pipeline_mode=pl.Buffered(1))


def _cast_kernel(x_ref, o_ref):
    o_ref[...] = x_ref[...].astype(o_ref.dtype)


def _to_bf16(w):
    cols = w.shape[-1]
    rows = w.size // cols
    block_rows = rows
    while block_rows * cols * w.dtype.itemsize > CAST_BLOCK_BYTES and block_rows % 32 == 0:
        block_rows //= 2
    out = pl.pallas_call(
        _cast_kernel,
        grid=(rows // block_rows,),
        in_specs=[pl.BlockSpec((block_rows, cols), lambda i: (i, 0))],
        out_specs=pl.BlockSpec((block_rows, cols), lambda i: (i, 0)),
        out_shape=jax.ShapeDtypeStruct((rows, cols), _BF16),
        compiler_params=pltpu.CompilerParams(dimension_semantics=("arbitrary",), vmem_limit_bytes=VMEM_LIMIT),
        name="cast_bf16",
    )(w.reshape(rows, cols))
    return out.reshape(w.shape)


def _ffn_rows(h, g_ref, win_ref, wout_ref, pre, post):
    xn = _rmsnorm(h, g_ref[pre:pre + 1, :]).astype(_BF16)
    gate = _dot(xn, win_ref[:, :D_FF])
    up = _dot(xn, win_ref[:, D_FF:])
    act = (gate * _sigmoid(gate) * up).astype(_BF16)
    y = _dot(act, wout_ref[...])
    return h + 0.5 * _rmsnorm(y, g_ref[post:post + 1, :])


def _ffn_kernel(pre, post, h_ref, hm_ref, g_ref, win_ref, wout_ref, o_ref, om_ref):
    @pl.when(pl.program_id(0) == 0)
    def _():
        om_ref[...] = _ffn_rows(hm_ref[...], g_ref, win_ref, wout_ref, pre, post)

    o_ref[...] = _ffn_rows(h_ref[...], g_ref, win_ref, wout_ref, pre, post)


def _ffn_call(h, hm, norm_g, w_in, w_out, layer, which, pre, post):
    rows = h.shape[0]
    row_spec = pl.BlockSpec((TILE_ROWS, D_MODEL), lambda i: (i, 0))
    meta_spec = _const_spec((N_META, D_MODEL), (0, 0))
    return pl.pallas_call(
        functools.partial(_ffn_kernel, pre, post),
        grid=(rows // TILE_ROWS,),
        in_specs=[
            row_spec,
            meta_spec,
            _const_spec((None,) + norm_g.shape[1:], (layer, 0, 0)),
            _const_spec((None, None) + w_in.shape[2:], (layer, which, 0, 0)),
            _const_spec((None, None) + w_out.shape[2:], (layer, which, 0, 0)),
        ],
        out_specs=[row_spec, pl.BlockSpec((N_META, D_MODEL), lambda i: (0, 0))],
        out_shape=[jax.ShapeDtypeStruct(h.shape, h.dtype), jax.ShapeDtypeStruct(hm.shape, hm.dtype)],
        compiler_params=pltpu.CompilerParams(
            dimension_semantics=("arbitrary",), vmem_limit_bytes=VMEM_LIMIT),
        name="ffn_half_step",
    )(h, hm, norm_g, w_in, w_out)


def _alibi_slope(head):
    return 2.0 ** (-8.0 * (head + 1) / N_HEADS)


def _mixer_kernel(layer, h_ref, hm_ref, g_ref, win_ref, wgrp_ref, pscale_ref, wpp_ref, sinks_ref, wap_ref, wo_ref,
                  o_ref, om_ref, ucarry_ref, kprev_ref, vprev_ref, kmeta_ref, vmeta_ref):
    i = pl.program_id(0)
    rd = lax.rem(i, 2)
    wr = 1 - rd

    def mixer_rows(h, is_meta):
        rows = h.shape[0]
        n_blocks = rows // BLOCK
        hn = _rmsnorm(h, g_ref[MIX_PRE:MIX_PRE + 1, :]).astype(_BF16)
        row_idx = lax.broadcasted_iota(jnp.int32, (rows, 1), 0)

        k = _dot(hn, win_ref[:, C_K:C_V]).astype(_BF16)
        v_t = _proj_t(win_ref[:, C_V:C_GP], hn).astype(_BF16)
        q_t = (_proj_t(win_ref[:, C_Q:C_K], hn) * (HEAD_DIM ** -0.5)).astype(_BF16)

        if is_meta:
            kmeta_ref[...] = k[:N_META, :]
            vmeta_ref[...] = v_t[:, :N_META]
        else:
            k_meta = kmeta_ref[...]
            v_meta_t = vmeta_ref[...]
            k_carry = kprev_ref[rd]
            v_carry_t = vprev_ref[rd]
            kprev_ref[wr] = k[rows - BLOCK:, :]
            vprev_ref[wr] = v_t[:, rows - BLOCK:]

        def pool_sums():
            u = _dot(hn, win_ref[:, C_U:C_Q])
            if is_meta:
                u = jnp.where(row_idx < N_META, u, 0.0)
                history = jnp.zeros((POOL_HALO, POOL_DIM), _F32)
                ucarry_ref[0] = u[:POOL_HALO, :]
                tok1 = jnp.minimum(row_idx + 1, POOL_WINDOWS[-1])
            else:
                history = ucarry_ref[rd]
                ucarry_ref[wr] = u[rows - POOL_HALO:, :]
            e = jnp.concatenate([history, u], axis=0)
            s2 = e + pltpu.roll(e, 1, 0)
            s4 = s2 + pltpu.roll(s2, 2, 0)
            s8 = s4 + pltpu.roll(s4, 4, 0)
            s16 = s8 + pltpu.roll(s8, 8, 0)
            mixed = []
            for gi, (win, s) in enumerate(zip(POOL_WINDOWS, (s2, s4, s8, s16))):
                lo = gi * POOL_GROUP_DIM
                cnt = jnp.minimum(tok1, win).astype(_F32) if is_meta else float(win)
                pooled = s[POOL_HALO:, lo:lo + POOL_GROUP_DIM] / cnt
                mixed.append((pooled - u[:, lo:lo + POOL_GROUP_DIM]).astype(_BF16))
            return mixed

        def pool_proj(mixed):
            mixed = [_dot(mx, wgrp_ref[gi]) for gi, mx in enumerate(mixed)]
            mixed = (jnp.concatenate(mixed, axis=-1) * pscale_ref[...]).astype(_BF16)
            return _dot(mixed, wpp_ref[...])

        def gate_chunk(c):
            return _sigmoid(_dot(hn, win_ref[:, C_GP + c * GATE_CHUNK:C_GP + (c + 1) * GATE_CHUNK]))

        n_gate_chunks = (C_END - C_GP) // GATE_CHUNK
        side = {}
        side_work = [lambda: side.__setitem__("mixed", pool_sums()),
                     lambda: side.__setitem__("y_pool", pool_proj(side["mixed"]))]
        side_work += [functools.partial(lambda c: side.__setitem__(("gate", c), gate_chunk(c)), c)
                      for c in range(n_gate_chunks)]

        q_idx = lax.broadcasted_iota(jnp.int32, (1, BLOCK), 1)
        if is_meta:
            key_idx = lax.broadcasted_iota(jnp.int32, (BLOCK, 1), 0)
            masks = [((key_idx <= q_idx) & (key_idx < N_META), None)]
        else:
            key_idx = lax.broadcasted_iota(jnp.int32, (2 * BLOCK + N_META, 1), 0)
            in_blk = jnp.bitwise_and(key_idx, BLOCK - 1)
            is_prev = key_idx < BLOCK
            is_meta_key = key_idx >= 2 * BLOCK
            dist = jnp.where(is_prev, q_idx + BLOCK - in_blk, q_idx - in_blk)
            ok = ((dist >= 0) & (dist < WINDOW)) | is_meta_key
            alibi_d = jnp.where(is_meta_key, 0, dist).astype(_F32)
            first_valid_key = jnp.where(i > 0, 0, BLOCK)
            masks = [(ok & (key_idx >= first_valid_key), alibi_d)] + [(ok, alibi_d)] * (n_blocks - 1)
        zeros_q = jnp.zeros((HEAD_DIM, GQA_GROUP * BLOCK), _BF16)

        def scores(b, kh):
            r0 = b * BLOCK
            lanes = slice((kh // 2) * 128, (kh // 2 + 1) * 128)
            if is_meta:
                kk = k[:, lanes]
            else:
                k_prv = k_carry if b == 0 else k[r0 - BLOCK:r0, :]
                kk = jnp.concatenate([k_prv[:, lanes], k[r0:r0 + BLOCK, lanes], k_meta[:, lanes]], axis=0)
            q_stack = jnp.concatenate(
                [q_t[(kh * GQA_GROUP + g) * HEAD_DIM:(kh * GQA_GROUP + g + 1) * HEAD_DIM, r0:r0 + BLOCK]
                 for g in range(GQA_GROUP)], axis=1)
            q_z = jnp.concatenate([q_stack, zeros_q] if kh % 2 == 0 else [zeros_q, q_stack], axis=0)
            return _dot(kk, q_z)

        def attend(b, kh, s_t):
            r0 = b * BLOCK
            ok_b, alibi_b = masks[b]
            rows_v = slice(kh * HEAD_DIM, (kh + 1) * HEAD_DIM)
            if is_meta:
                vv_t = v_t[rows_v, :]
            else:
                v_prv_t = v_carry_t if b == 0 else v_t[:, r0 - BLOCK:r0]
                vv_t = jnp.concatenate(
                    [v_prv_t[rows_v, :], v_t[rows_v, r0:r0 + BLOCK], v_meta_t[rows_v, :]], axis=1)
            probs, inv_l = [], []
            for g in range(GQA_GROUP):
                head = kh * GQA_GROUP + g
                s_h = s_t[:, g * BLOCK:(g + 1) * BLOCK]
                if alibi_b is not None:
                    s_h = s_h - _alibi_slope(head) * alibi_b
                logits = jnp.where(ok_b, s_h, NEG_INF)
                sink = sinks_ref[layer, head]
                m = jnp.maximum(jnp.max(logits, axis=0, keepdims=True), sink)
                p = jnp.exp(logits - m)
                denom = jnp.sum(p, axis=0, keepdims=True) + jnp.exp(sink - m)
                probs.append(p.astype(_BF16))
                inv_l.append(1.0 / denom)
            o_t = _dot(vv_t, jnp.concatenate(probs, axis=1))
            return [o_t[:, g * BLOCK:(g + 1) * BLOCK] * inv_l[g] for g in range(GQA_GROUP)]

        steps = [(b, kh) for b in range(n_blocks) for kh in range(N_KV_HEADS)]
        head_rows = {b: [] for b in range(n_blocks)}
        s_next = scores(*steps[0])
        for n, (b, kh) in enumerate(steps):
            s_cur = s_next
            if n + 1 < len(steps):
                s_next = scores(*steps[n + 1])
            head_rows[b] += attend(b, kh, s_cur)
            if n < len(side_work):
                side_work[n]()
        for work in side_work[len(steps):]:
            work()

        attn_t = jnp.concatenate(
            [jnp.concatenate(head_rows[b], axis=0) for b in range(n_blocks)], axis=1).astype(_BF16)
        y_attn = _dot_tn(attn_t, wap_ref[...])

        gates = jnp.concatenate([side[("gate", c)] for c in range(n_gate_chunks)], axis=-1)
        merged = (gates[:, :D_MODEL] * side["y_pool"] + gates[:, D_MODEL:] * y_attn).astype(_BF16)
        z = _dot(merged, wo_ref[...])
        return h + _rmsnorm(z, g_ref[MIX_POST:MIX_POST + 1, :])

    @pl.when(i == 0)
    def _():
        kprev_ref[0] = jnp.zeros((BLOCK, KV_DIM), _BF16)
        vprev_ref[0] = jnp.zeros((KV_DIM, BLOCK), _BF16)
        hm = jnp.concatenate([hm_ref[...], jnp.zeros((BLOCK - N_META, D_MODEL), _F32)], axis=0)
        om_ref[...] = mixer_rows(hm, True)[:N_META, :]

    o_ref[...] = mixer_rows(h_ref[...], False)


def _mixer_call(h, hm, norm_g, w_in, w_grp, pscale, w_pp, sinks, w_ap, w_o, layer):
    rows = h.shape[0]
    row_spec = pl.BlockSpec((TILE_ROWS, D_MODEL), lambda i: (i, 0))
    return pl.pallas_call(
        functools.partial(_mixer_kernel, layer),
        grid=(rows // TILE_ROWS,),
        in_specs=[
            row_spec,
            _const_spec((N_META, D_MODEL), (0, 0)),
            _const_spec((None,) + norm_g.shape[1:], (layer, 0, 0)),
            _const_spec((None,) + w_in.shape[1:], (layer, 0, 0)),
            _const_spec((None,) + w_grp.shape[1:], (layer, 0, 0, 0)),
            _const_spec((None,) + pscale.shape[1:], (layer, 0, 0)),
            _const_spec((None,) + w_pp.shape[1:], (layer, 0, 0)),
            pl.BlockSpec(memory_space=pltpu.SMEM),
            _const_spec((None,) + w_ap.shape[1:], (layer, 0, 0)),
            _const_spec((None,) + w_o.shape[1:], (layer, 0, 0)),
        ],
        out_specs=[row_spec, pl.BlockSpec((N_META, D_MODEL), lambda i: (0, 0))],
        out_shape=[jax.ShapeDtypeStruct(h.shape, h.dtype), jax.ShapeDtypeStruct(hm.shape, hm.dtype)],
        scratch_shapes=[
            pltpu.VMEM((2, POOL_HALO, POOL_DIM), _F32),
            pltpu.VMEM((2, BLOCK, KV_DIM), _BF16),
            pltpu.VMEM((2, KV_DIM, BLOCK), _BF16),
            pltpu.VMEM((N_META, KV_DIM), _BF16),
            pltpu.VMEM((KV_DIM, N_META), _BF16),
        ],
        compiler_params=pltpu.CompilerParams(
            dimension_semantics=("arbitrary",), vmem_limit_bytes=VMEM_LIMIT),
        name="gated_mixers",
    )(h, hm, norm_g, w_in, w_grp, pscale, w_pp, sinks, w_ap, w_o)


def kernel(x, meta_tokens, norm_g, ffn_w_in, ffn_w_out, w_in, pool_w_grp, pool_scale, w_pool_proj, sinks,
           w_attn_proj, w_o):
    batch, seq, d = x.shape
    assert batch == 1 and d == D_MODEL and seq % TILE_ROWS == 0 and meta_tokens.shape == (N_META, D_MODEL)
    assert POOL_HALO == N_META
    depth = norm_g.shape[0]
    ffn_w_in, ffn_w_out, w_in, pool_w_grp, w_pool_proj, w_attn_proj, w_o = (
        _to_bf16(w) for w in (ffn_w_in, ffn_w_out, w_in, pool_w_grp, w_pool_proj, w_attn_proj, w_o))
    pscale = pool_scale.reshape(depth, 1, POOL_DIM)
    h, hm = x[0], meta_tokens.astype(x.dtype)
    for l in range(depth):
        h, hm = _ffn_call(h, hm, norm_g, ffn_w_in, ffn_w_out, l, 0, FFN1_PRE, FFN1_POST)
        h, hm = _mixer_call(h, hm, norm_g, w_in, pool_w_grp, pscale, w_pool_proj, sinks, w_attn_proj, w_o, l)
        h, hm = _ffn_call(h, hm, norm_g, ffn_w_in, ffn_w_out, l, 1, FFN2_PRE, FFN2_POST)
    return h[None]
```

```python
import functools

import jax
import jax.numpy as jnp
from jax import lax
from jax.experimental import pallas as pl
from jax.experimental.pallas import tpu as pltpu

D_MODEL = 1024
N_META = 16
N_HEADS = 16
N_KV_HEADS = 4
HEAD_DIM = 64
GQA_GROUP = N_HEADS // N_KV_HEADS
WINDOW = 128
BLOCK = 128
POOL_WINDOWS = (2, 4, 8, 16)
POOL_GROUP_DIM = 128
POOL_DIM = 512
KV_DIM = 256
D_FF = 2816
RMS_EPS = 1e-6
NEG_INF = -1e30
FFN1_PRE, FFN1_POST, MIX_PRE, MIX_POST, FFN2_PRE, FFN2_POST = range(6)

C_U, C_Q, C_K, C_V, C_GP, C_GA, C_END = 0, 512, 1536, 1792, 2048, 3072, 4096

GATE_CHUNK = 256
POOL_HALO = 16
MIXER_TILE_ROWS = 512
POST_CHUNK = 256
FFN_TILE_ROWS = 1024
FFN_SUB_ROWS = 256
VMEM_LIMIT = 56 * 1024 * 1024
CAST_BLOCK_BYTES = 4 * 1024 * 1024

_BF16 = jnp.bfloat16
_F32 = jnp.float32


def _rmsnorm(x, g):
    ms = jnp.mean(x * x, axis=-1, keepdims=True)
    return x * lax.rsqrt(ms + RMS_EPS) * g


def _sigmoid(x):
    return 1.0 / (1.0 + jnp.exp(-x))


def _dot(a, b):
    return jnp.dot(a, b, preferred_element_type=_F32)


def _dot_tn(a, b):
    return lax.dot_general(a, b, (((0,), (0,)), ((), ())), preferred_element_type=_F32)


def _proj_t(w, x):
    return lax.dot_general(w, x, (((0,), (1,)), ((), ())), preferred_element_type=_F32)


def _const_spec(block_shape, index):
    return pl.BlockSpec(block_shape, lambda i: index, pipeline_mode=pl.Buffered(1))


def _cast_kernel(x_ref, o_ref):
    o_ref[...] = x_ref[...].astype(o_ref.dtype)


def _to_bf16(w):
    cols = w.shape[-1]
    rows = w.size // cols
    block_rows = rows
    while block_rows * cols * w.dtype.itemsize > CAST_BLOCK_BYTES and block_rows % 32 == 0:
        block_rows //= 2
    out = pl.pallas_call(
        _cast_kernel,
        grid=(rows // block_rows,),
        in_specs=[pl.BlockSpec((block_rows, cols), lambda i: (i, 0))],
        out_specs=pl.BlockSpec((block_rows, cols), lambda i: (i, 0)),
        out_shape=jax.ShapeDtypeStruct((rows, cols), _BF16),
        compiler_params=pltpu.CompilerParams(dimension_semantics=("arbitrary",), vmem_limit_bytes=VMEM_LIMIT),
        name="cast_bf16",
    )(w.reshape(rows, cols))
    return out.reshape(w.shape)


def _ffn_rows(h, g_ref, win_ref, wout_ref, pre, post):
    xn = _rmsnorm(h, g_ref[pre:pre + 1, :]).astype(_BF16)
    gate = _dot(xn, win_ref[:, :D_FF])
    up = _dot(xn, win_ref[:, D_FF:])
    act = (gate * _sigmoid(gate) * up).astype(_BF16)
    y = _dot(act, wout_ref[...])
    return h + 0.5 * _rmsnorm(y, g_ref[post:post + 1, :])


def _ffn_kernel(pre, post, h_ref, hm_ref, g_ref, win_ref, wout_ref, o_ref, om_ref):
    @pl.when(pl.program_id(0) == 0)
    def _():
        om_ref[...] = _ffn_rows(hm_ref[...], g_ref, win_ref, wout_ref, pre, post)

    for r0 in range(0, FFN_TILE_ROWS, FFN_SUB_ROWS):
        o_ref[r0:r0 + FFN_SUB_ROWS, :] = _ffn_rows(
            h_ref[r0:r0 + FFN_SUB_ROWS, :], g_ref, win_ref, wout_ref, pre, post)


def _ffn_call(h, hm, norm_g, w_in, w_out, layer, which, pre, post):
    rows = h.shape[0]
    row_spec = pl.BlockSpec((FFN_TILE_ROWS, D_MODEL), lambda i: (i, 0))
    meta_spec = _const_spec((N_META, D_MODEL), (0, 0))
    return pl.pallas_call(
        functools.partial(_ffn_kernel, pre, post),
        grid=(rows // FFN_TILE_ROWS,),
        in_specs=[
            row_spec,
            meta_spec,
            _const_spec((None,) + norm_g.shape[1:], (layer, 0, 0)),
            _const_spec((None, None) + w_in.shape[2:], (layer, which, 0, 0)),
            _const_spec((None, None) + w_out.shape[2:], (layer, which, 0, 0)),
        ],
        out_specs=[row_spec, pl.BlockSpec((N_META, D_MODEL), lambda i: (0, 0))],
        out_shape=[jax.ShapeDtypeStruct(h.shape, h.dtype), jax.ShapeDtypeStruct(hm.shape, hm.dtype)],
        compiler_params=pltpu.CompilerParams(
            dimension_semantics=("arbitrary",), vmem_limit_bytes=VMEM_LIMIT),
        name="ffn_half_step",
    )(h, hm, norm_g, w_in, w_out)


def _alibi_slope(head):
    return 2.0 ** (-8.0 * (head + 1) / N_HEADS)


def _mixer_kernel(layer, h_ref, hm_ref, g_ref, win_ref, wgrp_ref, pscale_ref, wpp_ref, sinks_ref, wap_ref, wo_ref,
                  o_ref, om_ref, ucarry_ref, kprev_ref, vprev_ref, kmeta_ref, vmeta_ref):
    i = pl.program_id(0)
    rd = lax.rem(i, 2)
    wr = 1 - rd

    def mixer_rows(h, is_meta, store):
        rows = h.shape[0]
        n_blocks = rows // BLOCK
        hn = _rmsnorm(h, g_ref[MIX_PRE:MIX_PRE + 1, :]).astype(_BF16)
        row_idx = lax.broadcasted_iota(jnp.int32, (rows, 1), 0)

        k = _dot(hn, win_ref[:, C_K:C_V]).astype(_BF16)
        v_t = _proj_t(win_ref[:, C_V:C_GP], hn).astype(_BF16)
        q_t = (_proj_t(win_ref[:, C_Q:C_K], hn) * (HEAD_DIM ** -0.5)).astype(_BF16)

        if is_meta:
            kmeta_ref[...] = k[:N_META, :]
            vmeta_ref[...] = v_t[:, :N_META]
        else:
            k_meta = kmeta_ref[...]
            v_meta_t = vmeta_ref[...]
            k_carry = kprev_ref[rd]
            v_carry_t = vprev_ref[rd]
            kprev_ref[wr] = k[rows - BLOCK:, :]
            vprev_ref[wr] = v_t[:, rows - BLOCK:]

        def pool_sums():
            u = _dot(hn, win_ref[:, C_U:C_Q])
            if is_meta:
                u = jnp.where(row_idx < N_META, u, 0.0)
                history = jnp.zeros((POOL_HALO, POOL_DIM), _F32)
                ucarry_ref[0] = u[:POOL_HALO, :]
                tok1 = jnp.minimum(row_idx + 1, POOL_WINDOWS[-1])
            else:
                history = ucarry_ref[rd]
                ucarry_ref[wr] = u[rows - POOL_HALO:, :]
            e = jnp.concatenate([history, u], axis=0)
            s2 = e + pltpu.roll(e, 1, 0)
            s4 = s2 + pltpu.roll(s2, 2, 0)
            s8 = s4 + pltpu.roll(s4, 4, 0)
            s16 = s8 + pltpu.roll(s8, 8, 0)
            mixed = []
            for gi, (win, s) in enumerate(zip(POOL_WINDOWS, (s2, s4, s8, s16))):
                lo = gi * POOL_GROUP_DIM
                cnt = jnp.minimum(tok1, win).astype(_F32) if is_meta else float(win)
                pooled = s[POOL_HALO:, lo:lo + POOL_GROUP_DIM] / cnt
                mixed.append((pooled - u[:, lo:lo + POOL_GROUP_DIM]).astype(_BF16))
            return mixed

        def pool_proj(mixed):
            mixed = [_dot(mx, wgrp_ref[gi]) for gi, mx in enumerate(mixed)]
            mixed = (jnp.concatenate(mixed, axis=-1) * pscale_ref[...]).astype(_BF16)
            return _dot(mixed, wpp_ref[...])

        def gate_chunk(c):
            return _sigmoid(_dot(hn, win_ref[:, C_GP + c * GATE_CHUNK:C_GP + (c + 1) * GATE_CHUNK]))

        n_gate_chunks = (C_END - C_GP) // GATE_CHUNK
        side = {}
        side_work = [lambda: side.__setitem__("mixed", pool_sums()),
                     lambda: side.__setitem__("y_pool", pool_proj(side["mixed"]))]
        side_work += [functools.partial(lambda c: side.__setitem__(("gate", c), gate_chunk(c)), c)
                      for c in range(n_gate_chunks)]

        q_idx = lax.broadcasted_iota(jnp.int32, (1, BLOCK), 1)
        if is_meta:
            key_idx = lax.broadcasted_iota(jnp.int32, (BLOCK, 1), 0)
            masks = [((key_idx <= q_idx) & (key_idx < N_META), None)]
        else:
            key_idx = lax.broadcasted_iota(jnp.int32, (2 * BLOCK + N_META, 1), 0)
            in_blk = jnp.bitwise_and(key_idx, BLOCK - 1)
            is_prev = key_idx < BLOCK
            is_meta_key = key_idx >= 2 * BLOCK
            dist = jnp.where(is_prev, q_idx + BLOCK - in_blk, q_idx - in_blk)
            ok = ((dist >= 0) & (dist < WINDOW)) | is_meta_key
            alibi_d = jnp.where(is_meta_key, 0, dist).astype(_F32)
            first_valid_key = jnp.where(i > 0, 0, BLOCK)
            masks = [(ok & (key_idx >= first_valid_key), alibi_d)] + [(ok, alibi_d)] * (n_blocks - 1)
        zeros_q = jnp.zeros((HEAD_DIM, GQA_GROUP * BLOCK), _BF16)

        def scores(b, kh):
            r0 = b * BLOCK
            lanes = slice((kh // 2) * 128, (kh // 2 + 1) * 128)
            if is_meta:
                kk = k[:, lanes]
            else:
                k_prv = k_carry if b == 0 else k[r0 - BLOCK:r0, :]
                kk = jnp.concatenate([k_prv[:, lanes], k[r0:r0 + BLOCK, lanes], k_meta[:, lanes]], axis=0)
            q_stack = jnp.concatenate(
                [q_t[(kh * GQA_GROUP + g) * HEAD_DIM:(kh * GQA_GROUP + g + 1) * HEAD_DIM, r0:r0 + BLOCK]
                 for g in range(GQA_GROUP)], axis=1)
            q_z = jnp.concatenate([q_stack, zeros_q] if kh % 2 == 0 else [zeros_q, q_stack], axis=0)
            return _dot(kk, q_z)

        def attend(b, kh, s_t):
            r0 = b * BLOCK
            ok_b, alibi_b = masks[b]
            rows_v = slice(kh * HEAD_DIM, (kh + 1) * HEAD_DIM)
            if is_meta:
                vv_t = v_t[rows_v, :]
            else:
                v_prv_t = v_carry_t if b == 0 else v_t[:, r0 - BLOCK:r0]
                vv_t = jnp.concatenate(
                    [v_prv_t[rows_v, :], v_t[rows_v, r0:r0 + BLOCK], v_meta_t[rows_v, :]], axis=1)
            probs, inv_l = [], []
            for g in range(GQA_GROUP):
                head = kh * GQA_GROUP + g
                s_h = s_t[:, g * BLOCK:(g + 1) * BLOCK]
                if alibi_b is not None:
                    s_h = s_h - _alibi_slope(head) * alibi_b
                logits = jnp.where(ok_b, s_h, NEG_INF)
                sink = sinks_ref[layer, head]
                m = jnp.maximum(jnp.max(logits, axis=0, keepdims=True), sink)
                p = jnp.exp(logits - m)
                denom = jnp.sum(p, axis=0, keepdims=True) + jnp.exp(sink - m)
                probs.append(p.astype(_BF16))
                inv_l.append(1.0 / denom)
            o_t = _dot(vv_t, jnp.concatenate(probs, axis=1))
            return [o_t[:, g * BLOCK:(g + 1) * BLOCK] * inv_l[g] for g in range(GQA_GROUP)]

        head_rows = {b: [] for b in range(n_blocks)}

        def post_work(b_lo, b_hi):
            r0, r1 = b_lo * BLOCK, b_hi * BLOCK
            st = {}

            def y_piece(c):
                if "attn_t" not in st:
                    st["attn_t"] = jnp.concatenate(
                        [jnp.concatenate(head_rows[b], axis=0) for b in range(b_lo, b_hi)], axis=1).astype(_BF16)
                st["y", c] = _dot_tn(st["attn_t"], wap_ref[:, c * POST_CHUNK:(c + 1) * POST_CHUNK])

            def z_piece(c):
                if "merged" not in st:
                    y_attn = jnp.concatenate([st["y", cc] for cc in range(n_post_chunks)], axis=-1)
                    g_pool = jnp.concatenate([side["gate", cc][r0:r1] for cc in range(n_gate_chunks // 2)], axis=-1)
                    g_attn = jnp.concatenate(
                        [side["gate", cc][r0:r1] for cc in range(n_gate_chunks // 2, n_gate_chunks)], axis=-1)
                    st["merged"] = (g_pool * side["y_pool"][r0:r1] + g_attn * y_attn).astype(_BF16)
                st["z", c] = _dot(st["merged"], wo_ref[:, c * POST_CHUNK:(c + 1) * POST_CHUNK])

            def finish():
                z = jnp.concatenate([st["z", cc] for cc in range(n_post_chunks)], axis=-1)
                store(r0, r1, h[r0:r1] + _rmsnorm(z, g_ref[MIX_POST:MIX_POST + 1, :]))

            n_post_chunks = D_MODEL // POST_CHUNK
            return ([functools.partial(y_piece, c) for c in range(n_post_chunks)]
                    + [functools.partial(z_piece, c) for c in range(n_post_chunks)] + [finish])

        n_groups = 2 if n_blocks % 2 == 0 else 1
        blocks_per_group = n_blocks // n_groups
        steps = [(b, kh) for b in range(n_blocks) for kh in range(N_KV_HEADS)]
        steps_per_group = blocks_per_group * N_KV_HEADS
        queue = list(side_work)
        queue.pop(0)()
        s_next = scores(*steps[0])
        for n, (b, kh) in enumerate(steps):
            s_cur = s_next
            if n + 1 < len(steps):
                s_next = scores(*steps[n + 1])
            head_rows[b] += attend(b, kh, s_cur)
            steps_left = steps_per_group - n % steps_per_group
            for _ in range(-(-len(queue) // steps_left)):
                queue.pop(0)()
            if steps_left == 1:
                grp = n // steps_per_group
                queue += post_work(grp * blocks_per_group, (grp + 1) * blocks_per_group)
        while queue:
            queue.pop(0)()

    def store_meta(r0, r1, value):
        om_ref[...] = value[:N_META, :]

    def store_rows(r0, r1, value):
        o_ref[r0:r1, :] = value

    @pl.when(i == 0)
    def _():
        kprev_ref[0] = jnp.zeros((BLOCK, KV_DIM), _BF16)
        vprev_ref[0] = jnp.zeros((KV_DIM, BLOCK), _BF16)
        hm = jnp.concatenate([hm_ref[...], jnp.zeros((BLOCK - N_META, D_MODEL), _F32)], axis=0)
        mixer_rows(hm, True, store_meta)

    mixer_rows(h_ref[...], False, store_rows)


def _mixer_call(h, hm, norm_g, w_in, w_grp, pscale, w_pp, sinks, w_ap, w_o, layer):
    rows = h.shape[0]
    row_spec = pl.BlockSpec((MIXER_TILE_ROWS, D_MODEL), lambda i: (i, 0))
    return pl.pallas_call(
        functools.partial(_mixer_kernel, layer),
        grid=(rows // MIXER_TILE_ROWS,),
        in_specs=[
            row_spec,
            _const_spec((N_META, D_MODEL), (0, 0)),
            _const_spec((None,) + norm_g.shape[1:], (layer, 0, 0)),
            _const_spec((None,) + w_in.shape[1:], (layer, 0, 0)),
            _const_spec((None,) + w_grp.shape[1:], (layer, 0, 0, 0)),
            _const_spec((None,) + pscale.shape[1:], (layer, 0, 0)),
            _const_spec((None,) + w_pp.shape[1:], (layer, 0, 0)),
            pl.BlockSpec(memory_space=pltpu.SMEM),
            _const_spec((None,) + w_ap.shape[1:], (layer, 0, 0)),
            _const_spec((None,) + w_o.shape[1:], (layer, 0, 0)),
        ],
        out_specs=[row_spec, pl.BlockSpec((N_META, D_MODEL), lambda i: (0, 0))],
        out_shape=[jax.ShapeDtypeStruct(h.shape, h.dtype), jax.ShapeDtypeStruct(hm.shape, hm.dtype)],
        scratch_shapes=[
            pltpu.VMEM((2, POOL_HALO, POOL_DIM), _F32),
            pltpu.VMEM((2, BLOCK, KV_DIM), _BF16),
            pltpu.VMEM((2, KV_DIM, BLOCK), _BF16),
            pltpu.VMEM((N_META, KV_DIM), _BF16),
            pltpu.VMEM((KV_DIM, N_META), _BF16),
        ],
        compiler_params=pltpu.CompilerParams(
            dimension_semantics=("arbitrary",), vmem_limit_bytes=VMEM_LIMIT),
        name="gated_mixers",
    )(h, hm, norm_g, w_in, w_grp, pscale, w_pp, sinks, w_ap, w_o)


def kernel(x, meta_tokens, norm_g, ffn_w_in, ffn_w_out, w_in, pool_w_grp, pool_scale, w_pool_proj, sinks,
           w_attn_proj, w_o):
    batch, seq, d = x.shape
    assert batch == 1 and d == D_MODEL and seq % FFN_TILE_ROWS == 0 and seq % MIXER_TILE_ROWS == 0 and meta_tokens.shape == (N_META, D_MODEL)
    assert POOL_HALO == N_META
    depth = norm_g.shape[0]
    ffn_w_in, ffn_w_out, w_in, pool_w_grp, w_pool_proj, w_attn_proj, w_o = (
        _to_bf16(w) for w in (ffn_w_in, ffn_w_out, w_in, pool_w_grp, w_pool_proj, w_attn_proj, w_o))
    pscale = pool_scale.reshape(depth, 1, POOL_DIM)
    h, hm = x[0], meta_tokens.astype(x.dtype)
    for l in range(depth):
        h, hm = _ffn_call(h, hm, norm_g, ffn_w_in, ffn_w_out, l, 0, FFN1_PRE, FFN1_POST)
        h, hm = _mixer_call(h, hm, norm_g, w_in, pool_w_grp, pscale, w_pool_proj, sinks, w_attn_proj, w_o, l)
        h, hm = _ffn_call(h, hm, norm_g, ffn_w_in, ffn_w_out, l, 1, FFN2_PRE, FFN2_POST)
    return h[None]
```

```python
import functools

import jax
import jax.numpy as jnp
from jax import lax
from jax.experimental import pallas as pl
from jax.experimental.pallas import tpu as pltpu

D_MODEL = 1024
N_META = 16
N_HEADS = 16
N_KV_HEADS = 4
HEAD_DIM = 64
GQA_GROUP = N_HEADS // N_KV_HEADS
WINDOW = 128
BLOCK = 128
POOL_WINDOWS = (2, 4, 8, 16)
POOL_GROUP_DIM = 128
POOL_DIM = 512
KV_DIM = 256
N_KEYS = 2 * BLOCK + N_META
D_FF = 2816
RMS_EPS = 1e-6
NEG_INF = -1e30
FFN1_PRE, FFN1_POST, MIX_PRE, MIX_POST, FFN2_PRE, FFN2_POST = range(6)

C_U, C_Q, C_K, C_V, C_GP, C_GA, C_END = 0, 512, 1536, 1792, 2048, 3072, 4096

GATE_CHUNK = 256
POOL_HALO = 16
MIXER_TILE_ROWS = 512
POST_CHUNK = 256
FFN_TILE_ROWS = 1024
FFN_SUB_ROWS = 256
VMEM_LIMIT = 56 * 1024 * 1024
FIRST_CAST_STEPS = 16
BF16_SUBLANES = 16

_BF16 = jnp.bfloat16
_F32 = jnp.float32


def _rmsnorm(x, g):
    ms = jnp.mean(x * x, axis=-1, keepdims=True)
    return x * lax.rsqrt(ms + RMS_EPS) * g


def _sigmoid(x):
    return 1.0 / (1.0 + jnp.exp(-x))


def _dot(a, b):
    return jnp.dot(a, b, preferred_element_type=_F32)


def _dot_tn(a, b):
    return lax.dot_general(a, b, (((0,), (0,)), ((), ())), preferred_element_type=_F32)


def _proj_t(w, x):
    return lax.dot_general(w, x, (((0,), (1,)), ((), ())), preferred_element_type=_F32)


def _const_spec(block_shape, index):
    return pl.BlockSpec(block_shape, lambda i: index, pipeline_mode=pl.Buffered(1))


def _cast_plan(w, index, n_steps):
    rows, cols = w.shape[len(index):]
    assert rows % (n_steps * BF16_SUBLANES) == 0, (w.shape, n_steps)
    block_rows = rows // n_steps
    in_spec = pl.BlockSpec((None,) * len(index) + (block_rows, cols), lambda i: tuple(index) + (i, 0))
    out_spec = pl.BlockSpec((block_rows, cols), lambda i: (i, 0))
    return in_spec, out_spec, jax.ShapeDtypeStruct((rows, cols), _BF16)


def _cast_blocks(src_refs, dst_refs):
    for src, dst in zip(src_refs, dst_refs):
        dst[...] = src[...].astype(dst.dtype)


def _cast_kernel(*refs):
    _cast_blocks(refs[:len(refs) // 2], refs[len(refs) // 2:])


def _cast_call(weights, n_steps):
    plans = [_cast_plan(w, index, n_steps) for w, index in weights]
    return pl.pallas_call(
        _cast_kernel,
        grid=(n_steps,),
        in_specs=[p[0] for p in plans],
        out_specs=[p[1] for p in plans],
        out_shape=[p[2] for p in plans],
        compiler_params=pltpu.CompilerParams(dimension_semantics=("arbitrary",), vmem_limit_bytes=VMEM_LIMIT),
        name="cast_bf16",
    )(*[w for w, _ in weights])


def _ffn_rows(h, g_ref, win_ref, wout_ref, pre, post):
    xn = _rmsnorm(h, g_ref[pre:pre + 1, :]).astype(_BF16)
    gate = _dot(xn, win_ref[:, :D_FF])
    up = _dot(xn, win_ref[:, D_FF:])
    act = (gate * _sigmoid(gate) * up).astype(_BF16)
    y = _dot(act, wout_ref[...])
    return h + 0.5 * _rmsnorm(y, g_ref[post:post + 1, :])


def _ffn_kernel(pre, post, n_cast, *refs):
    h_ref, hm_ref, g_ref, win_ref, wout_ref = refs[:5]
    o_ref, om_ref = refs[5 + n_cast:7 + n_cast]
    _cast_blocks(refs[5:5 + n_cast], refs[7 + n_cast:])

    @pl.when(pl.program_id(0) == 0)
    def _():
        om_ref[...] = _ffn_rows(hm_ref[...], g_ref, win_ref, wout_ref, pre, post)

    for r0 in range(0, FFN_TILE_ROWS, FFN_SUB_ROWS):
        o_ref[r0:r0 + FFN_SUB_ROWS, :] = _ffn_rows(
            h_ref[r0:r0 + FFN_SUB_ROWS, :], g_ref, win_ref, wout_ref, pre, post)


def _ffn_call(h, hm, norm_g, w_in, w_out, layer, pre, post, cast_next):
    n_steps = h.shape[0] // FFN_TILE_ROWS
    row_spec = pl.BlockSpec((FFN_TILE_ROWS, D_MODEL), lambda i: (i, 0))
    plans = [_cast_plan(w, index, n_steps) for w, index in cast_next]
    return pl.pallas_call(
        functools.partial(_ffn_kernel, pre, post, len(plans)),
        grid=(n_steps,),
        in_specs=[
            row_spec,
            _const_spec((N_META, D_MODEL), (0, 0)),
            _const_spec((None,) + norm_g.shape[1:], (layer, 0, 0)),
            _const_spec(w_in.shape, (0, 0)),
            _const_spec(w_out.shape, (0, 0)),
        ] + [p[0] for p in plans],
        out_specs=[row_spec, pl.BlockSpec((N_META, D_MODEL), lambda i: (0, 0))] + [p[1] for p in plans],
        out_shape=[jax.ShapeDtypeStruct(h.shape, h.dtype), jax.ShapeDtypeStruct(hm.shape, hm.dtype)]
        + [p[2] for p in plans],
        compiler_params=pltpu.CompilerParams(
            dimension_semantics=("arbitrary",), vmem_limit_bytes=VMEM_LIMIT),
        name="ffn_half_step",
    )(h, hm, norm_g, w_in, w_out, *[w for w, _ in cast_next])


def _alibi_slope(head):
    return 2.0 ** (-8.0 * (head + 1) / N_HEADS)


def _mixer_kernel(layer, n_cast, *refs):
    h_ref, hm_ref, g_ref, win_ref, wgrp_ref, pscale_ref, wpp_ref, sinks_ref, wap_ref, wo_ref = refs[:10]
    o_ref, om_ref = refs[10 + n_cast:12 + n_cast]
    ucarry_ref, kprev_ref, vprev_ref, kmeta_ref, vmeta_ref, bias_ref = refs[12 + 2 * n_cast:]
    _cast_blocks(refs[10:10 + n_cast], refs[12 + n_cast:12 + 2 * n_cast])
    i = pl.program_id(0)
    rd = lax.rem(i, 2)
    wr = 1 - rd

    def mixer_rows(h, is_meta, store):
        rows = h.shape[0]
        n_blocks = rows // BLOCK
        hn = _rmsnorm(h, g_ref[MIX_PRE:MIX_PRE + 1, :]).astype(_BF16)
        row_idx = lax.broadcasted_iota(jnp.int32, (rows, 1), 0)

        k = _dot(hn, win_ref[:, C_K:C_V]).astype(_BF16)
        v_t = _proj_t(win_ref[:, C_V:C_GP], hn).astype(_BF16)
        q_t = (_proj_t(win_ref[:, C_Q:C_K], hn) * (HEAD_DIM ** -0.5)).astype(_BF16)

        if is_meta:
            kmeta_ref[...] = k[:N_META, :]
            vmeta_ref[...] = v_t[:, :N_META]
        else:
            k_meta = kmeta_ref[...]
            v_meta_t = vmeta_ref[...]
            k_carry = kprev_ref[rd]
            v_carry_t = vprev_ref[rd]
            kprev_ref[wr] = k[rows - BLOCK:, :]
            vprev_ref[wr] = v_t[:, rows - BLOCK:]

        def pool_sums():
            u = _dot(hn, win_ref[:, C_U:C_Q])
            if is_meta:
                u = jnp.where(row_idx < N_META, u, 0.0)
                history = jnp.zeros((POOL_HALO, POOL_DIM), _F32)
                ucarry_ref[0] = u[:POOL_HALO, :]
                tok1 = jnp.minimum(row_idx + 1, POOL_WINDOWS[-1])
            else:
                history = ucarry_ref[rd]
                ucarry_ref[wr] = u[rows - POOL_HALO:, :]
            e = jnp.concatenate([history, u], axis=0)
            s2 = e + pltpu.roll(e, 1, 0)
            s4 = s2 + pltpu.roll(s2, 2, 0)
            s8 = s4 + pltpu.roll(s4, 4, 0)
            s16 = s8 + pltpu.roll(s8, 8, 0)
            mixed = []
            for gi, (win, s) in enumerate(zip(POOL_WINDOWS, (s2, s4, s8, s16))):
                lo = gi * POOL_GROUP_DIM
                cnt = jnp.minimum(tok1, win).astype(_F32) if is_meta else float(win)
                pooled = s[POOL_HALO:, lo:lo + POOL_GROUP_DIM] / cnt
                mixed.append((pooled - u[:, lo:lo + POOL_GROUP_DIM]).astype(_BF16))
            return mixed

        def pool_proj(mixed):
            mixed = [_dot(mx, wgrp_ref[gi]) for gi, mx in enumerate(mixed)]
            mixed = (jnp.concatenate(mixed, axis=-1) * pscale_ref[...]).astype(_BF16)
            return _dot(mixed, wpp_ref[...])

        def gate_chunk(c):
            return _sigmoid(_dot(hn, win_ref[:, C_GP + c * GATE_CHUNK:C_GP + (c + 1) * GATE_CHUNK]))

        n_gate_chunks = (C_END - C_GP) // GATE_CHUNK
        side = {}
        side_work = [lambda: side.__setitem__("mixed", pool_sums()),
                     lambda: side.__setitem__("y_pool", pool_proj(side["mixed"]))]
        side_work += [functools.partial(lambda c: side.__setitem__(("gate", c), gate_chunk(c)), c)
                      for c in range(n_gate_chunks)]

        q_idx = lax.broadcasted_iota(jnp.int32, (1, BLOCK), 1)
        if is_meta:
            key_idx = lax.broadcasted_iota(jnp.int32, (BLOCK, 1), 0)
            masks = [((key_idx <= q_idx) & (key_idx < N_META), None)]
        else:
            key_idx = lax.broadcasted_iota(jnp.int32, (N_KEYS, 1), 0)
            first_block_ok = key_idx >= jnp.where(i > 0, 0, BLOCK)
            masks = [(first_block_ok, None)] + [(None, None)] * (n_blocks - 1)
        zeros_q = jnp.zeros((HEAD_DIM, GQA_GROUP * BLOCK), _BF16)

        def scores(b, kh):
            r0 = b * BLOCK
            lanes = slice((kh // 2) * 128, (kh // 2 + 1) * 128)
            if is_meta:
                kk = k[:, lanes]
            else:
                k_prv = k_carry if b == 0 else k[r0 - BLOCK:r0, :]
                kk = jnp.concatenate([k_prv[:, lanes], k[r0:r0 + BLOCK, lanes], k_meta[:, lanes]], axis=0)
            q_stack = jnp.concatenate(
                [q_t[(kh * GQA_GROUP + g) * HEAD_DIM:(kh * GQA_GROUP + g + 1) * HEAD_DIM, r0:r0 + BLOCK]
                 for g in range(GQA_GROUP)], axis=1)
            q_z = jnp.concatenate([q_stack, zeros_q] if kh % 2 == 0 else [zeros_q, q_stack], axis=0)
            return _dot(kk, q_z)

        def attend(b, kh, s_t):
            r0 = b * BLOCK
            ok_b, _ = masks[b]
            rows_v = slice(kh * HEAD_DIM, (kh + 1) * HEAD_DIM)
            if is_meta:
                vv_t = v_t[rows_v, :]
            else:
                v_prv_t = v_carry_t if b == 0 else v_t[:, r0 - BLOCK:r0]
                vv_t = jnp.concatenate(
                    [v_prv_t[rows_v, :], v_t[rows_v, r0:r0 + BLOCK], v_meta_t[rows_v, :]], axis=1)
            probs, inv_l = [], []
            for g in range(GQA_GROUP):
                head = kh * GQA_GROUP + g
                logits = s_t[:, g * BLOCK:(g + 1) * BLOCK]
                if not is_meta:
                    logits = logits + bias_ref[head]
                if ok_b is not None:
                    logits = jnp.where(ok_b, logits, NEG_INF)
                sink = sinks_ref[layer, head]
                m = jnp.maximum(jnp.max(logits, axis=0, keepdims=True), sink)
                p = jnp.exp(logits - m)
                denom = jnp.sum(p, axis=0, keepdims=True) + jnp.exp(sink - m)
                probs.append(p.astype(_BF16))
                inv_l.append(1.0 / denom)
            o_t = _dot(vv_t, jnp.concatenate(probs, axis=1))
            return [o_t[:, g * BLOCK:(g + 1) * BLOCK] * inv_l[g] for g in range(GQA_GROUP)]

        head_rows = {b: [] for b in range(n_blocks)}

        def post_work(b_lo, b_hi):
            r0, r1 = b_lo * BLOCK, b_hi * BLOCK
            st = {}

            def y_piece(c):
                if "attn_t" not in st:
                    st["attn_t"] = jnp.concatenate(
                        [jnp.concatenate(head_rows[b], axis=0) for b in range(b_lo, b_hi)], axis=1).astype(_BF16)
                st["y", c] = _dot_tn(st["attn_t"], wap_ref[:, c * POST_CHUNK:(c + 1) * POST_CHUNK])

            def z_piece(c):
                if "merged" not in st:
                    y_attn = jnp.concatenate([st["y", cc] for cc in range(n_post_chunks)], axis=-1)
                    g_pool = jnp.concatenate([side["gate", cc][r0:r1] for cc in range(n_gate_chunks // 2)], axis=-1)
                    g_attn = jnp.concatenate(
                        [side["gate", cc][r0:r1] for cc in range(n_gate_chunks // 2, n_gate_chunks)], axis=-1)
                    st["merged"] = (g_pool * side["y_pool"][r0:r1] + g_attn * y_attn).astype(_BF16)
                st["z", c] = _dot(st["merged"], wo_ref[:, c * POST_CHUNK:(c + 1) * POST_CHUNK])

            def finish():
                z = jnp.concatenate([st["z", cc] for cc in range(n_post_chunks)], axis=-1)
                store(r0, r1, h[r0:r1] + _rmsnorm(z, g_ref[MIX_POST:MIX_POST + 1, :]))

            n_post_chunks = D_MODEL // POST_CHUNK
            return ([functools.partial(y_piece, c) for c in range(n_post_chunks)]
                    + [functools.partial(z_piece, c) for c in range(n_post_chunks)] + [finish])

        n_groups = 2 if n_blocks % 2 == 0 else 1
        blocks_per_group = n_blocks // n_groups
        steps = [(b, kh) for b in range(n_blocks) for kh in range(N_KV_HEADS)]
        steps_per_group = blocks_per_group * N_KV_HEADS
        queue = list(side_work)
        queue.pop(0)()
        s_next = scores(*steps[0])
        for n, (b, kh) in enumerate(steps):
            s_cur = s_next
            if n + 1 < len(steps):
                s_next = scores(*steps[n + 1])
            head_rows[b] += attend(b, kh, s_cur)
            steps_left = steps_per_group - n % steps_per_group
            for _ in range(-(-len(queue) // steps_left)):
                queue.pop(0)()
            if steps_left == 1:
                grp = n // steps_per_group
                queue += post_work(grp * blocks_per_group, (grp + 1) * blocks_per_group)
        while queue:
            queue.pop(0)()

    def store_meta(r0, r1, value):
        om_ref[...] = value[:N_META, :]

    def store_rows(r0, r1, value):
        o_ref[r0:r1, :] = value

    @pl.when(i == 0)
    def _():
        kprev_ref[0] = jnp.zeros((BLOCK, KV_DIM), _BF16)
        vprev_ref[0] = jnp.zeros((KV_DIM, BLOCK), _BF16)
        key_idx = lax.broadcasted_iota(jnp.int32, (N_KEYS, 1), 0)
        q_idx = lax.broadcasted_iota(jnp.int32, (1, BLOCK), 1)
        in_blk = jnp.bitwise_and(key_idx, BLOCK - 1)
        is_meta_key = key_idx >= 2 * BLOCK
        dist = jnp.where(key_idx < BLOCK, q_idx + BLOCK - in_blk, q_idx - in_blk)
        ok = ((dist >= 0) & (dist < WINDOW)) | is_meta_key
        alibi_d = jnp.where(is_meta_key, 0, dist).astype(_F32)
        for head in range(N_HEADS):
            bias_ref[head] = jnp.where(ok, -_alibi_slope(head) * alibi_d, NEG_INF)
        hm = jnp.concatenate([hm_ref[...], jnp.zeros((BLOCK - N_META, D_MODEL), _F32)], axis=0)
        mixer_rows(hm, True, store_meta)

    mixer_rows(h_ref[...], False, store_rows)


def _mixer_call(h, hm, norm_g, w_in, w_grp, pscale, w_pp, sinks, w_ap, w_o, layer, cast_next):
    n_steps = h.shape[0] // MIXER_TILE_ROWS
    row_spec = pl.BlockSpec((MIXER_TILE_ROWS, D_MODEL), lambda i: (i, 0))
    plans = [_cast_plan(w, index, n_steps) for w, index in cast_next]
    return pl.pallas_call(
        functools.partial(_mixer_kernel, layer, len(plans)),
        grid=(n_steps,),
        in_specs=[
            row_spec,
            _const_spec((N_META, D_MODEL), (0, 0)),
            _const_spec((None,) + norm_g.shape[1:], (layer, 0, 0)),
            _const_spec(w_in.shape, (0, 0)),
            _const_spec(w_grp.shape, (0, 0, 0)),
            _const_spec((None,) + pscale.shape[1:], (layer, 0, 0)),
            _const_spec(w_pp.shape, (0, 0)),
            pl.BlockSpec(memory_space=pltpu.SMEM),
            _const_spec(w_ap.shape, (0, 0)),
            _const_spec(w_o.shape, (0, 0)),
        ] + [p[0] for p in plans],
        out_specs=[row_spec, pl.BlockSpec((N_META, D_MODEL), lambda i: (0, 0))] + [p[1] for p in plans],
        out_shape=[jax.ShapeDtypeStruct(h.shape, h.dtype), jax.ShapeDtypeStruct(hm.shape, hm.dtype)]
        + [p[2] for p in plans],
        scratch_shapes=[
            pltpu.VMEM((2, POOL_HALO, POOL_DIM), _F32),
            pltpu.VMEM((2, BLOCK, KV_DIM), _BF16),
            pltpu.VMEM((2, KV_DIM, BLOCK), _BF16),
            pltpu.VMEM((N_META, KV_DIM), _BF16),
            pltpu.VMEM((KV_DIM, N_META), _BF16),
            pltpu.VMEM((N_HEADS, N_KEYS, BLOCK), _F32),
        ],
        compiler_params=pltpu.CompilerParams(
            dimension_semantics=("arbitrary",), vmem_limit_bytes=VMEM_LIMIT),
        name="gated_mixers",
    )(h, hm, norm_g, w_in, w_grp, pscale, w_pp, sinks, w_ap, w_o, *[w for w, _ in cast_next])


def kernel(x, meta_tokens, norm_g, ffn_w_in, ffn_w_out, w_in, pool_w_grp, pool_scale, w_pool_proj, sinks,
           w_attn_proj, w_o):
    batch, seq, d = x.shape
    assert batch == 1 and d == D_MODEL and seq % FFN_TILE_ROWS == 0 and seq % MIXER_TILE_ROWS == 0
    assert meta_tokens.shape == (N_META, D_MODEL) and POOL_HALO == N_META
    depth = norm_g.shape[0]
    n_groups = len(POOL_WINDOWS)
    ffn_w_out_v = ffn_w_out.reshape(depth, 2, 2 * D_FF, D_MODEL // 2)
    w_grp_v = pool_w_grp.reshape(depth, n_groups * POOL_GROUP_DIM, POOL_GROUP_DIM)
    pscale = pool_scale.reshape(depth, 1, POOL_DIM)

    def ffn_weights(l, j):
        return [(ffn_w_in, (l, j)), (ffn_w_out_v, (l, j))]

    def mixer_weights(l):
        return [(w_in, (l,)), (w_grp_v, (l,)), (w_pool_proj, (l,)), (w_attn_proj, (l,)), (w_o, (l,))]

    h, hm = x[0], meta_tokens.astype(x.dtype)
    w_ffn = _cast_call(ffn_weights(0, 0), FIRST_CAST_STEPS)
    for l in range(depth):
        h, hm, *w_mix = _ffn_call(h, hm, norm_g, w_ffn[0], w_ffn[1].reshape(D_FF, D_MODEL), l,
                                  FFN1_PRE, FFN1_POST, mixer_weights(l))
        w_mix[1] = w_mix[1].reshape(n_groups, POOL_GROUP_DIM, POOL_GROUP_DIM)
        h, hm, *w_ffn = _mixer_call(h, hm, norm_g, w_mix[0], w_mix[1], pscale, w_mix[2], sinks, w_mix[3], w_mix[4],
                                    l, ffn_weights(l, 1))
        h, hm, *w_next = _ffn_call(h, hm, norm_g, w_ffn[0], w_ffn[1].reshape(D_FF, D_MODEL), l,
                                   FFN2_PRE, FFN2_POST, ffn_weights(l + 1, 0) if l + 1 < depth else [])
        w_ffn = w_next
    return h[None]
```

```python
import functools

import jax
import jax.numpy as jnp
from jax import lax
from jax.experimental import pallas as pl
from jax.experimental.pallas import tpu as pltpu

D_MODEL = 1024
N_META = 16
N_HEADS = 16
N_KV_HEADS = 4
HEAD_DIM = 64
GQA_GROUP = N_HEADS // N_KV_HEADS
WINDOW = 128
BLOCK = 128
POOL_WINDOWS = (2, 4, 8, 16)
POOL_GROUP_DIM = 128
POOL_DIM = 512
KV_DIM = 256
N_KEYS = 2 * BLOCK + N_META
D_FF = 2816
RMS_EPS = 1e-6
NEG_INF = -1e30
FFN1_PRE, FFN1_POST, MIX_PRE, MIX_POST, FFN2_PRE, FFN2_POST = range(6)

C_U, C_Q, C_K, C_V, C_GP, C_GA, C_END = 0, 512, 1536, 1792, 2048, 3072, 4096

GATE_CHUNK = 256
POOL_HALO = 16
MIXER_TILE_ROWS = 512
POST_CHUNK = 256
FFN_TILE_ROWS = 1024
FFN_SUB_ROWS = 256
VMEM_LIMIT = 56 * 1024 * 1024
FIRST_CAST_STEPS = 16
BF16_SUBLANES = 16

_BF16 = jnp.bfloat16
_F32 = jnp.float32


def _rmsnorm(x, g):
    ms = jnp.mean(x * x, axis=-1, keepdims=True)
    return x * lax.rsqrt(ms + RMS_EPS) * g


def _sigmoid(x):
    return 1.0 / (1.0 + jnp.exp(-x))


def _dot(a, b):
    return jnp.dot(a, b, preferred_element_type=_F32)


def _dot_tn(a, b):
    return lax.dot_general(a, b, (((0,), (0,)), ((), ())), preferred_element_type=_F32)


def _proj_t(w, x):
    return lax.dot_general(w, x, (((0,), (1,)), ((), ())), preferred_element_type=_F32)


def _const_spec(block_shape, index):
    return pl.BlockSpec(block_shape, lambda i: index, pipeline_mode=pl.Buffered(1))


def _cast_plan(w, index, n_steps):
    rows, cols = w.shape[len(index):]
    n_blocks = max(d for d in range(1, n_steps + 1) if n_steps % d == 0 and rows % (d * BF16_SUBLANES) == 0)
    block_rows, revisit = rows // n_blocks, n_steps // n_blocks
    in_spec = pl.BlockSpec((None,) * len(index) + (block_rows, cols), lambda i: tuple(index) + (i // revisit, 0))
    out_spec = pl.BlockSpec((block_rows, cols), lambda i: (i // revisit, 0))
    return in_spec, out_spec, jax.ShapeDtypeStruct((rows, cols), _BF16)


def _cast_blocks(src_refs, dst_refs):
    for src, dst in zip(src_refs, dst_refs):
        dst[...] = src[...].astype(dst.dtype)


def _cast_kernel(*refs):
    _cast_blocks(refs[:len(refs) // 2], refs[len(refs) // 2:])


def _cast_call(weights, n_steps):
    plans = [_cast_plan(w, index, n_steps) for w, index in weights]
    return pl.pallas_call(
        _cast_kernel,
        grid=(n_steps,),
        in_specs=[p[0] for p in plans],
        out_specs=[p[1] for p in plans],
        out_shape=[p[2] for p in plans],
        compiler_params=pltpu.CompilerParams(dimension_semantics=("arbitrary",), vmem_limit_bytes=VMEM_LIMIT),
        name="cast_bf16",
    )(*[w for w, _ in weights])


def _ffn_rows(h, g_ref, win_ref, wout_ref, pre, post):
    xn = _rmsnorm(h, g_ref[pre:pre + 1, :]).astype(_BF16)
    gate = _dot(xn, win_ref[:, :D_FF])
    up = _dot(xn, win_ref[:, D_FF:])
    act = (gate * _sigmoid(gate) * up).astype(_BF16)
    y = _dot(act, wout_ref[...])
    return h + 0.5 * _rmsnorm(y, g_ref[post:post + 1, :])


def _ffn_kernel(pre, post, n_cast, *refs):
    h_ref, hm_ref, g_ref, win_ref, wout_ref = refs[:5]
    o_ref, om_ref = refs[5 + n_cast:7 + n_cast]
    _cast_blocks(refs[5:5 + n_cast], refs[7 + n_cast:])

    @pl.when(pl.program_id(0) == 0)
    def _():
        om_ref[...] = _ffn_rows(hm_ref[...], g_ref, win_ref, wout_ref, pre, post)

    for r0 in range(0, FFN_TILE_ROWS, FFN_SUB_ROWS):
        o_ref[r0:r0 + FFN_SUB_ROWS, :] = _ffn_rows(
            h_ref[r0:r0 + FFN_SUB_ROWS, :], g_ref, win_ref, wout_ref, pre, post)


def _ffn_call(h, hm, norm_g, w_in, w_out, layer, pre, post, cast_next):
    n_steps = h.shape[0] // FFN_TILE_ROWS
    row_spec = pl.BlockSpec((FFN_TILE_ROWS, D_MODEL), lambda i: (i, 0))
    plans = [_cast_plan(w, index, n_steps) for w, index in cast_next]
    return pl.pallas_call(
        functools.partial(_ffn_kernel, pre, post, len(plans)),
        grid=(n_steps,),
        in_specs=[
            row_spec,
            _const_spec((N_META, D_MODEL), (0, 0)),
            _const_spec((None,) + norm_g.shape[1:], (layer, 0, 0)),
            _const_spec(w_in.shape, (0, 0)),
            _const_spec(w_out.shape, (0, 0)),
        ] + [p[0] for p in plans],
        out_specs=[row_spec, pl.BlockSpec((N_META, D_MODEL), lambda i: (0, 0))] + [p[1] for p in plans],
        out_shape=[jax.ShapeDtypeStruct(h.shape, h.dtype), jax.ShapeDtypeStruct(hm.shape, hm.dtype)]
        + [p[2] for p in plans],
        compiler_params=pltpu.CompilerParams(
            dimension_semantics=("arbitrary",), vmem_limit_bytes=VMEM_LIMIT),
        name="ffn_half_step",
    )(h, hm, norm_g, w_in, w_out, *[w for w, _ in cast_next])


def _alibi_slope(head):
    return 2.0 ** (-8.0 * (head + 1) / N_HEADS)


def _mixer_kernel(layer, n_cast, *refs):
    h_ref, hm_ref, g_ref, win_ref, wgrp_ref, pscale_ref, wpp_ref, sinks_ref, wap_ref, wo_ref = refs[:10]
    o_ref, om_ref = refs[10 + n_cast:12 + n_cast]
    ucarry_ref, kprev_ref, vprev_ref, kmeta_ref, vmeta_ref, bias_ref = refs[12 + 2 * n_cast:]
    _cast_blocks(refs[10:10 + n_cast], refs[12 + n_cast:12 + 2 * n_cast])
    i = pl.program_id(0)
    rd = lax.rem(i, 2)
    wr = 1 - rd

    def mixer_rows(h, is_meta, store):
        rows = h.shape[0]
        n_blocks = rows // BLOCK
        hn = _rmsnorm(h, g_ref[MIX_PRE:MIX_PRE + 1, :]).astype(_BF16)
        row_idx = lax.broadcasted_iota(jnp.int32, (rows, 1), 0)

        k = _dot(hn, win_ref[:, C_K:C_V]).astype(_BF16)
        v_t = _proj_t(win_ref[:, C_V:C_GP], hn).astype(_BF16)
        q_t = (_proj_t(win_ref[:, C_Q:C_K], hn) * (HEAD_DIM ** -0.5)).astype(_BF16)

        if is_meta:
            kmeta_ref[...] = k[:N_META, :]
            vmeta_ref[...] = v_t[:, :N_META]
        else:
            k_meta = kmeta_ref[...]
            v_meta_t = vmeta_ref[...]
            k_carry = kprev_ref[rd]
            v_carry_t = vprev_ref[rd]
            kprev_ref[wr] = k[rows - BLOCK:, :]
            vprev_ref[wr] = v_t[:, rows - BLOCK:]

        def pool_sums():
            u = _dot(hn, win_ref[:, C_U:C_Q])
            if is_meta:
                u = jnp.where(row_idx < N_META, u, 0.0)
                history = jnp.zeros((POOL_HALO, POOL_DIM), _F32)
                ucarry_ref[0] = u[:POOL_HALO, :]
                tok1 = jnp.minimum(row_idx + 1, POOL_WINDOWS[-1])
            else:
                history = ucarry_ref[rd]
                ucarry_ref[wr] = u[rows - POOL_HALO:, :]
            e = jnp.concatenate([history, u], axis=0)
            s2 = e + pltpu.roll(e, 1, 0)
            s4 = s2 + pltpu.roll(s2, 2, 0)
            s8 = s4 + pltpu.roll(s4, 4, 0)
            s16 = s8 + pltpu.roll(s8, 8, 0)
            mixed = []
            for gi, (win, s) in enumerate(zip(POOL_WINDOWS, (s2, s4, s8, s16))):
                lo = gi * POOL_GROUP_DIM
                cnt = jnp.minimum(tok1, win).astype(_F32) if is_meta else float(win)
                pooled = s[POOL_HALO:, lo:lo + POOL_GROUP_DIM] / cnt
                mixed.append((pooled - u[:, lo:lo + POOL_GROUP_DIM]).astype(_BF16))
            return mixed

        def pool_proj(mixed):
            mixed = [_dot(mx, wgrp_ref[gi]) for gi, mx in enumerate(mixed)]
            mixed = (jnp.concatenate(mixed, axis=-1) * pscale_ref[...]).astype(_BF16)
            return _dot(mixed, wpp_ref[...])

        def gate_chunk(c):
            return _sigmoid(_dot(hn, win_ref[:, C_GP + c * GATE_CHUNK:C_GP + (c + 1) * GATE_CHUNK]))

        n_gate_chunks = (C_END - C_GP) // GATE_CHUNK
        side = {}
        side_work = [lambda: side.__setitem__("mixed", pool_sums()),
                     lambda: side.__setitem__("y_pool", pool_proj(side["mixed"]))]
        side_work += [functools.partial(lambda c: side.__setitem__(("gate", c), gate_chunk(c)), c)
                      for c in range(n_gate_chunks)]

        q_idx = lax.broadcasted_iota(jnp.int32, (1, BLOCK), 1)
        if is_meta:
            key_idx = lax.broadcasted_iota(jnp.int32, (BLOCK, 1), 0)
            masks = [((key_idx <= q_idx) & (key_idx < N_META), None)]
        else:
            key_idx = lax.broadcasted_iota(jnp.int32, (N_KEYS, 1), 0)
            first_block_ok = key_idx >= jnp.where(i > 0, 0, BLOCK)
            masks = [(first_block_ok, None)] + [(None, None)] * (n_blocks - 1)
        zeros_q = jnp.zeros((HEAD_DIM, GQA_GROUP * BLOCK), _BF16)

        def scores(b, kh):
            r0 = b * BLOCK
            lanes = slice((kh // 2) * 128, (kh // 2 + 1) * 128)
            if is_meta:
                kk = k[:, lanes]
            else:
                k_prv = k_carry if b == 0 else k[r0 - BLOCK:r0, :]
                kk = jnp.concatenate([k_prv[:, lanes], k[r0:r0 + BLOCK, lanes], k_meta[:, lanes]], axis=0)
            q_stack = jnp.concatenate(
                [q_t[(kh * GQA_GROUP + g) * HEAD_DIM:(kh * GQA_GROUP + g + 1) * HEAD_DIM, r0:r0 + BLOCK]
                 for g in range(GQA_GROUP)], axis=1)
            q_z = jnp.concatenate([q_stack, zeros_q] if kh % 2 == 0 else [zeros_q, q_stack], axis=0)
            return _dot(kk, q_z)

        def attend(b, kh, s_t):
            r0 = b * BLOCK
            ok_b, _ = masks[b]
            rows_v = slice(kh * HEAD_DIM, (kh + 1) * HEAD_DIM)
            if is_meta:
                vv_t = v_t[rows_v, :]
            else:
                v_prv_t = v_carry_t if b == 0 else v_t[:, r0 - BLOCK:r0]
                vv_t = jnp.concatenate(
                    [v_prv_t[rows_v, :], v_t[rows_v, r0:r0 + BLOCK], v_meta_t[rows_v, :]], axis=1)
            probs, inv_l = [], []
            for g in range(GQA_GROUP):
                head = kh * GQA_GROUP + g
                logits = s_t[:, g * BLOCK:(g + 1) * BLOCK]
                if not is_meta:
                    logits = logits + bias_ref[head]
                if ok_b is not None:
                    logits = jnp.where(ok_b, logits, NEG_INF)
                sink = sinks_ref[layer, head]
                m = jnp.maximum(jnp.max(logits, axis=0, keepdims=True), sink)
                p = jnp.exp(logits - m)
                denom = jnp.sum(p, axis=0, keepdims=True) + jnp.exp(sink - m)
                probs.append(p.astype(_BF16))
                inv_l.append(1.0 / denom)
            o_t = _dot(vv_t, jnp.concatenate(probs, axis=1))
            return [o_t[:, g * BLOCK:(g + 1) * BLOCK] * inv_l[g] for g in range(GQA_GROUP)]

        head_rows = {b: [] for b in range(n_blocks)}

        def post_work(b_lo, b_hi):
            r0, r1 = b_lo * BLOCK, b_hi * BLOCK
            st = {}

            def y_piece(c):
                if "attn_t" not in st:
                    st["attn_t"] = jnp.concatenate(
                        [jnp.concatenate(head_rows[b], axis=0) for b in range(b_lo, b_hi)], axis=1).astype(_BF16)
                st["y", c] = _dot_tn(st["attn_t"], wap_ref[:, c * POST_CHUNK:(c + 1) * POST_CHUNK])

            def z_piece(c):
                if "merged" not in st:
                    y_attn = jnp.concatenate([st["y", cc] for cc in range(n_post_chunks)], axis=-1)
                    g_pool = jnp.concatenate([side["gate", cc][r0:r1] for cc in range(n_gate_chunks // 2)], axis=-1)
                    g_attn = jnp.concatenate(
                        [side["gate", cc][r0:r1] for cc in range(n_gate_chunks // 2, n_gate_chunks)], axis=-1)
                    st["merged"] = (g_pool * side["y_pool"][r0:r1] + g_attn * y_attn).astype(_BF16)
                st["z", c] = _dot(st["merged"], wo_ref[:, c * POST_CHUNK:(c + 1) * POST_CHUNK])

            def finish():
                z = jnp.concatenate([st["z", cc] for cc in range(n_post_chunks)], axis=-1)
                store(r0, r1, h[r0:r1] + _rmsnorm(z, g_ref[MIX_POST:MIX_POST + 1, :]))

            n_post_chunks = D_MODEL // POST_CHUNK
            return ([functools.partial(y_piece, c) for c in range(n_post_chunks)]
                    + [functools.partial(z_piece, c) for c in range(n_post_chunks)] + [finish])

        n_groups = 2 if n_blocks % 2 == 0 else 1
        blocks_per_group = n_blocks // n_groups
        steps = [(b, kh) for b in range(n_blocks) for kh in range(N_KV_HEADS)]
        steps_per_group = blocks_per_group * N_KV_HEADS
        queue = list(side_work)
        queue.pop(0)()
        s_next = scores(*steps[0])
        for n, (b, kh) in enumerate(steps):
            s_cur = s_next
            if n + 1 < len(steps):
                s_next = scores(*steps[n + 1])
            head_rows[b] += attend(b, kh, s_cur)
            steps_left = steps_per_group - n % steps_per_group
            for _ in range(-(-len(queue) // steps_left)):
                queue.pop(0)()
            if steps_left == 1:
                grp = n // steps_per_group
                queue += post_work(grp * blocks_per_group, (grp + 1) * blocks_per_group)
        while queue:
            queue.pop(0)()

    def store_meta(r0, r1, value):
        om_ref[...] = value[:N_META, :]

    def store_rows(r0, r1, value):
        o_ref[r0:r1, :] = value

    @pl.when(i == 0)
    def _():
        kprev_ref[0] = jnp.zeros((BLOCK, KV_DIM), _BF16)
        vprev_ref[0] = jnp.zeros((KV_DIM, BLOCK), _BF16)
        key_idx = lax.broadcasted_iota(jnp.int32, (N_KEYS, 1), 0)
        q_idx = lax.broadcasted_iota(jnp.int32, (1, BLOCK), 1)
        in_blk = jnp.bitwise_and(key_idx, BLOCK - 1)
        is_meta_key = key_idx >= 2 * BLOCK
        dist = jnp.where(key_idx < BLOCK, q_idx + BLOCK - in_blk, q_idx - in_blk)
        ok = ((dist >= 0) & (dist < WINDOW)) | is_meta_key
        alibi_d = jnp.where(is_meta_key, 0, dist).astype(_F32)
        for head in range(N_HEADS):
            bias_ref[head] = jnp.where(ok, -_alibi_slope(head) * alibi_d, NEG_INF)
        hm = jnp.concatenate([hm_ref[...], jnp.zeros((BLOCK - N_META, D_MODEL), _F32)], axis=0)
        mixer_rows(hm, True, store_meta)

    mixer_rows(h_ref[...], False, store_rows)


def _mixer_call(h, hm, norm_g, w_in, w_grp, pscale, w_pp, sinks, w_ap, w_o, layer, cast_next):
    n_steps = h.shape[0] // MIXER_TILE_ROWS
    row_spec = pl.BlockSpec((MIXER_TILE_ROWS, D_MODEL), lambda i: (i, 0))
    plans = [_cast_plan(w, index, n_steps) for w, index in cast_next]
    return pl.pallas_call(
        functools.partial(_mixer_kernel, layer, len(plans)),
        grid=(n_steps,),
        in_specs=[
            row_spec,
            _const_spec((N_META, D_MODEL), (0, 0)),
            _const_spec((None,) + norm_g.shape[1:], (layer, 0, 0)),
            _const_spec(w_in.shape, (0, 0)),
            _const_spec(w_grp.shape, (0, 0, 0)),
            _const_spec((None,) + pscale.shape[1:], (layer, 0, 0)),
            _const_spec(w_pp.shape, (0, 0)),
            pl.BlockSpec(memory_space=pltpu.SMEM),
            _const_spec(w_ap.shape, (0, 0)),
            _const_spec(w_o.shape, (0, 0)),
        ] + [p[0] for p in plans],
        out_specs=[row_spec, pl.BlockSpec((N_META, D_MODEL), lambda i: (0, 0))] + [p[1] for p in plans],
        out_shape=[jax.ShapeDtypeStruct(h.shape, h.dtype), jax.ShapeDtypeStruct(hm.shape, hm.dtype)]
        + [p[2] for p in plans],
        scratch_shapes=[
            pltpu.VMEM((2, POOL_HALO, POOL_DIM), _F32),
            pltpu.VMEM((2, BLOCK, KV_DIM), _BF16),
            pltpu.VMEM((2, KV_DIM, BLOCK), _BF16),
            pltpu.VMEM((N_META, KV_DIM), _BF16),
            pltpu.VMEM((KV_DIM, N_META), _BF16),
            pltpu.VMEM((N_HEADS, N_KEYS, BLOCK), _F32),
        ],
        compiler_params=pltpu.CompilerParams(
            dimension_semantics=("arbitrary",), vmem_limit_bytes=VMEM_LIMIT),
        name="gated_mixers",
    )(h, hm, norm_g, w_in, w_grp, pscale, w_pp, sinks, w_ap, w_o, *[w for w, _ in cast_next])


def kernel(x, meta_tokens, norm_g, ffn_w_in, ffn_w_out, w_in, pool_w_grp, pool_scale, w_pool_proj, sinks,
           w_attn_proj, w_o):
    batch, seq, d = x.shape
    assert batch == 1 and d == D_MODEL and seq % FFN_TILE_ROWS == 0 and seq % MIXER_TILE_ROWS == 0
    assert meta_tokens.shape == (N_META, D_MODEL) and POOL_HALO == N_META
    depth = norm_g.shape[0]
    n_groups = len(POOL_WINDOWS)
    w_grp_v = pool_w_grp.reshape(depth, n_groups * POOL_GROUP_DIM, POOL_GROUP_DIM)
    pscale = pool_scale.reshape(depth, 1, POOL_DIM)

    def ffn_weights(l, j):
        return [(ffn_w_in, (l, j)), (ffn_w_out, (l, j))]

    def mixer_weights(l):
        return [(w_in, (l,)), (w_grp_v, (l,)), (w_pool_proj, (l,)), (w_attn_proj, (l,)), (w_o, (l,))]

    h, hm = x[0], meta_tokens.astype(x.dtype)
    w_ffn = _cast_call(ffn_weights(0, 0), FIRST_CAST_STEPS)
    for l in range(depth):
        h, hm, *w_mix = _ffn_call(h, hm, norm_g, w_ffn[0], w_ffn[1], l,
                                  FFN1_PRE, FFN1_POST, mixer_weights(l))
        w_mix[1] = w_mix[1].reshape(n_groups, POOL_GROUP_DIM, POOL_GROUP_DIM)
        h, hm, *w_ffn = _mixer_call(h, hm, norm_g, w_mix[0], w_mix[1], pscale, w_mix[2], sinks, w_mix[3], w_mix[4],
                                    l, ffn_weights(l, 1))
        h, hm, *w_next = _ffn_call(h, hm, norm_g, w_ffn[0], w_ffn[1], l,
                                   FFN2_PRE, FFN2_POST, ffn_weights(l + 1, 0) if l + 1 < depth else [])
        w_ffn = w_next
    return h[None]
```

```python
import functools

import jax
import jax.numpy as jnp
from jax import lax
from jax.experimental import pallas as pl
from jax.experimental.pallas import tpu as pltpu

D_MODEL = 1024
N_META = 16
N_HEADS = 16
N_KV_HEADS = 4
HEAD_DIM = 64
GQA_GROUP = N_HEADS // N_KV_HEADS
WINDOW = 128
BLOCK = 128
POOL_WINDOWS = (2, 4, 8, 16)
POOL_GROUP_DIM = 128
POOL_DIM = 512
KV_DIM = 256
N_KEYS = 2 * BLOCK + N_META
D_FF = 2816
RMS_EPS = 1e-6
NEG_INF = -1e30
FFN1_PRE, FFN1_POST, MIX_PRE, MIX_POST, FFN2_PRE, FFN2_POST = range(6)

C_U, C_Q, C_K, C_V, C_GP, C_GA, C_END = 0, 512, 1536, 1792, 2048, 3072, 4096

GATE_CHUNK = 256
POOL_HALO = 16
MIXER_TILE_ROWS = 512
POST_CHUNK = 256
FFN_TILE_ROWS = 1024
FFN_SUB_ROWS = 256
VMEM_LIMIT = 56 * 1024 * 1024
FIRST_CAST_STEPS = 16
BF16_SUBLANES = 16

_BF16 = jnp.bfloat16
_F32 = jnp.float32


def _rmsnorm(x, g):
    ms = jnp.mean(x * x, axis=-1, keepdims=True)
    return x * lax.rsqrt(ms + RMS_EPS) * g


def _sigmoid(x):
    return 1.0 / (1.0 + jnp.exp(-x))


def _dot(a, b):
    return jnp.dot(a, b, preferred_element_type=_F32)


def _dot_tn(a, b):
    return lax.dot_general(a, b, (((0,), (0,)), ((), ())), preferred_element_type=_F32)


def _proj_t(w, x):
    return lax.dot_general(w, x, (((0,), (1,)), ((), ())), preferred_element_type=_F32)


def _const_spec(block_shape, index):
    return pl.BlockSpec(block_shape, lambda i: index, pipeline_mode=pl.Buffered(1))


def _cast_plan(w, index, n_steps):
    rows, cols = w.shape[len(index):]
    n_blocks = max(d for d in range(1, n_steps + 1) if n_steps % d == 0 and rows % (d * BF16_SUBLANES) == 0)
    block_rows, revisit = rows // n_blocks, n_steps // n_blocks
    in_spec = pl.BlockSpec((None,) * len(index) + (block_rows, cols), lambda i: tuple(index) + (i // revisit, 0))
    out_spec = pl.BlockSpec((block_rows, cols), lambda i: (i // revisit, 0))
    return in_spec, out_spec, jax.ShapeDtypeStruct((rows, cols), _BF16)


def _cast_blocks(src_refs, dst_refs):
    for src, dst in zip(src_refs, dst_refs):
        dst[...] = src[...].astype(dst.dtype)


def _cast_kernel(*refs):
    _cast_blocks(refs[:len(refs) // 2], refs[len(refs) // 2:])


def _cast_call(weights, n_steps):
    plans = [_cast_plan(w, index, n_steps) for w, index in weights]
    return pl.pallas_call(
        _cast_kernel,
        grid=(n_steps,),
        in_specs=[p[0] for p in plans],
        out_specs=[p[1] for p in plans],
        out_shape=[p[2] for p in plans],
        compiler_params=pltpu.CompilerParams(dimension_semantics=("arbitrary",), vmem_limit_bytes=VMEM_LIMIT),
        name="cast_bf16",
    )(*[w for w, _ in weights])


def _ffn_rows(h, g_ref, win_ref, wout_ref, pre, post):
    xn = _rmsnorm(h, g_ref[pre:pre + 1, :]).astype(_BF16)
    gate = _dot(xn, win_ref[:, :D_FF])
    up = _dot(xn, win_ref[:, D_FF:])
    act = (gate * _sigmoid(gate) * up).astype(_BF16)
    y = _dot(act, wout_ref[...])
    return h + 0.5 * _rmsnorm(y, g_ref[post:post + 1, :])


def _ffn_kernel(pre, post, n_cast, *refs):
    h_ref, hm_ref, g_ref, win_ref, wout_ref = refs[:5]
    o_ref, om_ref = refs[5 + n_cast:7 + n_cast]
    _cast_blocks(refs[5:5 + n_cast], refs[7 + n_cast:])

    @pl.when(pl.program_id(0) == 0)
    def _():
        om_ref[...] = _ffn_rows(hm_ref[...], g_ref, win_ref, wout_ref, pre, post)

    for r0 in range(0, FFN_TILE_ROWS, FFN_SUB_ROWS):
        o_ref[r0:r0 + FFN_SUB_ROWS, :] = _ffn_rows(
            h_ref[r0:r0 + FFN_SUB_ROWS, :], g_ref, win_ref, wout_ref, pre, post)


def _ffn_call(h, hm, norm_g, w_in, w_out, layer, pre, post, cast_next):
    n_steps = h.shape[0] // FFN_TILE_ROWS
    row_spec = pl.BlockSpec((FFN_TILE_ROWS, D_MODEL), lambda i: (i, 0))
    plans = [_cast_plan(w, index, n_steps) for w, index in cast_next]
    return pl.pallas_call(
        functools.partial(_ffn_kernel, pre, post, len(plans)),
        grid=(n_steps,),
        in_specs=[
            row_spec,
            _const_spec((N_META, D_MODEL), (0, 0)),
            _const_spec((None,) + norm_g.shape[1:], (layer, 0, 0)),
            _const_spec(w_in.shape, (0, 0)),
            _const_spec(w_out.shape, (0, 0)),
        ] + [p[0] for p in plans],
        out_specs=[row_spec, pl.BlockSpec((N_META, D_MODEL), lambda i: (0, 0))] + [p[1] for p in plans],
        out_shape=[jax.ShapeDtypeStruct(h.shape, h.dtype), jax.ShapeDtypeStruct(hm.shape, hm.dtype)]
        + [p[2] for p in plans],
        compiler_params=pltpu.CompilerParams(
            dimension_semantics=("arbitrary",), vmem_limit_bytes=VMEM_LIMIT),
        name="ffn_half_step",
    )(h, hm, norm_g, w_in, w_out, *[w for w, _ in cast_next])


def _alibi_slope(head):
    return 2.0 ** (-8.0 * (head + 1) / N_HEADS)


def _mixer_kernel(layer, n_cast, *refs):
    h_ref, hm_ref, g_ref, win_ref, wgrp_ref, pscale_ref, wpp_ref, sinks_ref, wap_ref, wo_ref = refs[:10]
    o_ref, om_ref = refs[10 + n_cast:12 + n_cast]
    ucarry_ref, kprev_ref, vprev_ref, kmeta_ref, vmeta_ref, bias_ref = refs[12 + 2 * n_cast:]
    _cast_blocks(refs[10:10 + n_cast], refs[12 + n_cast:12 + 2 * n_cast])
    i = pl.program_id(0)
    rd = lax.rem(i, 2)
    wr = 1 - rd

    def mixer_rows(h, is_meta, store):
        rows = h.shape[0]
        n_blocks = rows // BLOCK
        hn = _rmsnorm(h, g_ref[MIX_PRE:MIX_PRE + 1, :]).astype(_BF16)
        row_idx = lax.broadcasted_iota(jnp.int32, (rows, 1), 0)

        k = _dot(hn, win_ref[:, C_K:C_V]).astype(_BF16)
        v_t = _proj_t(win_ref[:, C_V:C_GP], hn).astype(_BF16)
        q_t = (_proj_t(win_ref[:, C_Q:C_K], hn) * (HEAD_DIM ** -0.5)).astype(_BF16)

        if is_meta:
            kmeta_ref[...] = k[:N_META, :]
            vmeta_ref[...] = v_t[:, :N_META]
        else:
            k_meta = kmeta_ref[...]
            v_meta_t = vmeta_ref[...]
            k_carry = kprev_ref[rd]
            v_carry_t = vprev_ref[rd]
            kprev_ref[wr] = k[rows - BLOCK:, :]
            vprev_ref[wr] = v_t[:, rows - BLOCK:]

        def pool_sums():
            u = _dot(hn, win_ref[:, C_U:C_Q])
            if is_meta:
                u = jnp.where(row_idx < N_META, u, 0.0)
                history = jnp.zeros((POOL_HALO, POOL_DIM), _F32)
                ucarry_ref[0] = u[:POOL_HALO, :]
                tok1 = jnp.minimum(row_idx + 1, POOL_WINDOWS[-1])
            else:
                history = ucarry_ref[rd]
                ucarry_ref[wr] = u[rows - POOL_HALO:, :]
            e = jnp.concatenate([history, u], axis=0)
            s2 = e + pltpu.roll(e, 1, 0)
            s4 = s2 + pltpu.roll(s2, 2, 0)
            s8 = s4 + pltpu.roll(s4, 4, 0)
            s16 = s8 + pltpu.roll(s8, 8, 0)
            mixed = []
            for gi, (win, s) in enumerate(zip(POOL_WINDOWS, (s2, s4, s8, s16))):
                lo = gi * POOL_GROUP_DIM
                cnt = jnp.minimum(tok1, win).astype(_F32) if is_meta else float(win)
                pooled = s[POOL_HALO:, lo:lo + POOL_GROUP_DIM] / cnt
                mixed.append((pooled - u[:, lo:lo + POOL_GROUP_DIM]).astype(_BF16))
            return mixed

        def pool_proj(mixed):
            mixed = [_dot(mx, wgrp_ref[gi]) for gi, mx in enumerate(mixed)]
            mixed = (jnp.concatenate(mixed, axis=-1) * pscale_ref[...]).astype(_BF16)
            return _dot(mixed, wpp_ref[...])

        def gate_chunk(c):
            return _sigmoid(_dot(hn, win_ref[:, C_GP + c * GATE_CHUNK:C_GP + (c + 1) * GATE_CHUNK]))

        n_gate_chunks = (C_END - C_GP) // GATE_CHUNK
        side = {}
        side_work = [lambda: side.__setitem__("mixed", pool_sums()),
                     lambda: side.__setitem__("y_pool", pool_proj(side["mixed"]))]
        side_work += [functools.partial(lambda c: side.__setitem__(("gate", c), gate_chunk(c)), c)
                      for c in range(n_gate_chunks)]

        q_idx = lax.broadcasted_iota(jnp.int32, (1, BLOCK), 1)
        if is_meta:
            key_idx = lax.broadcasted_iota(jnp.int32, (BLOCK, 1), 0)
            masks = [((key_idx <= q_idx) & (key_idx < N_META), None)]
        else:
            key_idx = lax.broadcasted_iota(jnp.int32, (N_KEYS, 1), 0)
            first_block_ok = key_idx >= jnp.where(i > 0, 0, BLOCK)
            masks = [(first_block_ok, None)] + [(None, None)] * (n_blocks - 1)
        zeros_q = jnp.zeros((HEAD_DIM, GQA_GROUP * BLOCK), _BF16)

        def scores(b, kh):
            r0 = b * BLOCK
            lanes = slice((kh // 2) * 128, (kh // 2 + 1) * 128)
            if is_meta:
                kk = k[:, lanes]
            else:
                k_prv = k_carry if b == 0 else k[r0 - BLOCK:r0, :]
                kk = jnp.concatenate([k_prv[:, lanes], k[r0:r0 + BLOCK, lanes], k_meta[:, lanes]], axis=0)
            q_stack = jnp.concatenate(
                [q_t[(kh * GQA_GROUP + g) * HEAD_DIM:(kh * GQA_GROUP + g + 1) * HEAD_DIM, r0:r0 + BLOCK]
                 for g in range(GQA_GROUP)], axis=1)
            q_z = jnp.concatenate([q_stack, zeros_q] if kh % 2 == 0 else [zeros_q, q_stack], axis=0)
            return _dot(kk, q_z)

        def attend(b, kh, s_t):
            r0 = b * BLOCK
            ok_b, _ = masks[b]
            rows_v = slice(kh * HEAD_DIM, (kh + 1) * HEAD_DIM)
            if is_meta:
                vv_t = v_t[rows_v, :]
            else:
                v_prv_t = v_carry_t if b == 0 else v_t[:, r0 - BLOCK:r0]
                vv_t = jnp.concatenate(
                    [v_prv_t[rows_v, :], v_t[rows_v, r0:r0 + BLOCK], v_meta_t[rows_v, :]], axis=1)
            probs, inv_l = [], []
            for g in range(GQA_GROUP):
                head = kh * GQA_GROUP + g
                logits = s_t[:, g * BLOCK:(g + 1) * BLOCK]
                if not is_meta:
                    logits = logits + bias_ref[head]
                if ok_b is not None:
                    logits = jnp.where(ok_b, logits, NEG_INF)
                sink = sinks_ref[layer, head]
                m = jnp.maximum(jnp.max(logits, axis=0, keepdims=True), sink)
                p = jnp.exp(logits - m)
                denom = jnp.sum(p, axis=0, keepdims=True) + jnp.exp(sink - m)
                probs.append(p.astype(_BF16))
                inv_l.append(1.0 / denom)
            o_t = _dot(vv_t, jnp.concatenate(probs, axis=1))
            return [o_t[:, g * BLOCK:(g + 1) * BLOCK] * inv_l[g] for g in range(GQA_GROUP)]

        head_rows = {b: [] for b in range(n_blocks)}

        def post_work(b_lo, b_hi):
            r0, r1 = b_lo * BLOCK, b_hi * BLOCK
            st = {}

            def y_piece(c):
                if "attn_t" not in st:
                    st["attn_t"] = jnp.concatenate(
                        [jnp.concatenate(head_rows[b], axis=0) for b in range(b_lo, b_hi)], axis=1).astype(_BF16)
                st["y", c] = _dot_tn(st["attn_t"], wap_ref[:, c * POST_CHUNK:(c + 1) * POST_CHUNK])

            def z_piece(c):
                if "merged" not in st:
                    y_attn = jnp.concatenate([st["y", cc] for cc in range(n_post_chunks)], axis=-1)
                    g_pool = jnp.concatenate([side["gate", cc][r0:r1] for cc in range(n_gate_chunks // 2)], axis=-1)
                    g_attn = jnp.concatenate(
                        [side["gate", cc][r0:r1] for cc in range(n_gate_chunks // 2, n_gate_chunks)], axis=-1)
                    st["merged"] = (g_pool * side["y_pool"][r0:r1] + g_attn * y_attn).astype(_BF16)
                st["z", c] = _dot(st["merged"], wo_ref[:, c * POST_CHUNK:(c + 1) * POST_CHUNK])

            def finish():
                z = jnp.concatenate([st["z", cc] for cc in range(n_post_chunks)], axis=-1)
                store(r0, r1, h[r0:r1] + _rmsnorm(z, g_ref[MIX_POST:MIX_POST + 1, :]))

            n_post_chunks = D_MODEL // POST_CHUNK
            return ([functools.partial(y_piece, c) for c in range(n_post_chunks)]
                    + [functools.partial(z_piece, c) for c in range(n_post_chunks)] + [finish])

        n_groups = 2 if n_blocks % 2 == 0 else 1
        blocks_per_group = n_blocks // n_groups
        steps = [(b, kh) for b in range(n_blocks) for kh in range(N_KV_HEADS)]
        steps_per_group = blocks_per_group * N_KV_HEADS
        queue = list(side_work)
        queue.pop(0)()
        s_next = scores(*steps[0])
        for n, (b, kh) in enumerate(steps):
            s_cur = s_next
            if n + 1 < len(steps):
                s_next = scores(*steps[n + 1])
            head_rows[b] += attend(b, kh, s_cur)
            steps_left = steps_per_group - n % steps_per_group
            for _ in range(-(-len(queue) // steps_left)):
                queue.pop(0)()
            if steps_left == 1:
                grp = n // steps_per_group
                queue += post_work(grp * blocks_per_group, (grp + 1) * blocks_per_group)
        while queue:
            queue.pop(0)()

    def store_meta(r0, r1, value):
        om_ref[...] = value[:N_META, :]

    def store_rows(r0, r1, value):
        o_ref[r0:r1, :] = value

    @pl.when(i == 0)
    def _():
        kprev_ref[0] = jnp.zeros((BLOCK, KV_DIM), _BF16)
        vprev_ref[0] = jnp.zeros((KV_DIM, BLOCK), _BF16)
        key_idx = lax.broadcasted_iota(jnp.int32, (N_KEYS, 1), 0)
        q_idx = lax.broadcasted_iota(jnp.int32, (1, BLOCK), 1)
        in_blk = jnp.bitwise_and(key_idx, BLOCK - 1)
        is_meta_key = key_idx >= 2 * BLOCK
        dist = jnp.where(key_idx < BLOCK, q_idx + BLOCK - in_blk, q_idx - in_blk)
        ok = ((dist >= 0) & (dist < WINDOW)) | is_meta_key
        alibi_d = jnp.where(is_meta_key, 0, dist).astype(_F32)
        for head in range(N_HEADS):
            bias_ref[head] = jnp.where(ok, -_alibi_slope(head) * alibi_d, NEG_INF)
        hm = jnp.concatenate([hm_ref[...], jnp.zeros((BLOCK - N_META, D_MODEL), _F32)], axis=0)
        mixer_rows(hm, True, store_meta)

    mixer_rows(h_ref[...], False, store_rows)


def _mixer_call(h, hm, norm_g, w_in, w_grp, pscale, w_pp, sinks, w_ap, w_o, layer, cast_next):
    n_steps = h.shape[0] // MIXER_TILE_ROWS
    row_spec = pl.BlockSpec((MIXER_TILE_ROWS, D_MODEL), lambda i: (i, 0))
    plans = [_cast_plan(w, index, n_steps) for w, index in cast_next]
    return pl.pallas_call(
        functools.partial(_mixer_kernel, layer, len(plans)),
        grid=(n_steps,),
        in_specs=[
            row_spec,
            _const_spec((N_META, D_MODEL), (0, 0)),
            _const_spec((None,) + norm_g.shape[1:], (layer, 0, 0)),
            _const_spec(w_in.shape, (0, 0)),
            _const_spec(w_grp.shape, (0, 0, 0)),
            _const_spec((None,) + pscale.shape[1:], (layer, 0, 0)),
            _const_spec(w_pp.shape, (0, 0)),
            pl.BlockSpec(memory_space=pltpu.SMEM),
            _const_spec(w_ap.shape, (0, 0)),
            _const_spec(w_o.shape, (0, 0)),
        ] + [p[0] for p in plans],
        out_specs=[row_spec, pl.BlockSpec((N_META, D_MODEL), lambda i: (0, 0))] + [p[1] for p in plans],
        out_shape=[jax.ShapeDtypeStruct(h.shape, h.dtype), jax.ShapeDtypeStruct(hm.shape, hm.dtype)]
        + [p[2] for p in plans],
        scratch_shapes=[
            pltpu.VMEM((2, POOL_HALO, POOL_DIM), _F32),
            pltpu.VMEM((2, BLOCK, KV_DIM), _BF16),
            pltpu.VMEM((2, KV_DIM, BLOCK), _BF16),
            pltpu.VMEM((N_META, KV_DIM), _BF16),
            pltpu.VMEM((KV_DIM, N_META), _BF16),
            pltpu.VMEM((N_HEADS, N_KEYS, BLOCK), _F32),
        ],
        compiler_params=pltpu.CompilerParams(
            dimension_semantics=("arbitrary",), vmem_limit_bytes=VMEM_LIMIT),
        name="gated_mixers",
    )(h, hm, norm_g, w_in, w_grp, pscale, w_pp, sinks, w_ap, w_o, *[w for w, _ in cast_next])


def kernel(x, meta_tokens, norm_g, ffn_w_in, ffn_w_out, w_in, pool_w_grp, pool_scale, w_pool_proj, sinks,
           w_attn_proj, w_o):
    batch, seq, d = x.shape
    assert batch == 1 and d == D_MODEL and seq % FFN_TILE_ROWS == 0 and seq % MIXER_TILE_ROWS == 0
    assert meta_tokens.shape == (N_META, D_MODEL) and POOL_HALO == N_META
    depth = norm_g.shape[0]
    n_groups = len(POOL_WINDOWS)
    w_grp_v = pool_w_grp.reshape(depth, n_groups * POOL_GROUP_DIM, POOL_GROUP_DIM)
    pscale = pool_scale.reshape(depth, 1, POOL_DIM)

    def ffn_weights(l, j):
        return [(ffn_w_in, (l, j)), (ffn_w_out, (l, j))]

    def mixer_weights(l):
        return [(w_in, (l,)), (w_grp_v, (l,)), (w_pool_proj, (l,)), (w_attn_proj, (l,)), (w_o, (l,))]

    h, hm = x[0], meta_tokens.astype(x.dtype)
    w_ffn1 = _cast_call(ffn_weights(0, 0), FIRST_CAST_STEPS)
    w_mix = None
    for l in range(depth):
        last = l + 1 == depth
        h, hm, *cast = _ffn_call(h, hm, norm_g, w_ffn1[0], w_ffn1[1], l, FFN1_PRE, FFN1_POST,
                                 mixer_weights(l) if w_mix is None else [])
        w_mix = cast or w_mix
        w_grp = w_mix[1].reshape(n_groups, POOL_GROUP_DIM, POOL_GROUP_DIM)
        ahead = ffn_weights(l, 1) + ([] if last else ffn_weights(l + 1, 0) + mixer_weights(l + 1))
        h, hm, *cast = _mixer_call(h, hm, norm_g, w_mix[0], w_grp, pscale, w_mix[2], sinks, w_mix[3], w_mix[4],
                                   l, ahead)
        h, hm = _ffn_call(h, hm, norm_g, cast[0], cast[1], l, FFN2_PRE, FFN2_POST, [])
        w_ffn1, w_mix = cast[2:4], cast[4:]
    return h[None]
```

```python
import functools

import jax
import jax.numpy as jnp
from jax import lax
from jax.experimental import pallas as pl
from jax.experimental.pallas import tpu as pltpu

D_MODEL = 1024
N_META = 16
N_HEADS = 16
N_KV_HEADS = 4
HEAD_DIM = 64
GQA_GROUP = N_HEADS // N_KV_HEADS
WINDOW = 128
BLOCK = 128
POOL_WINDOWS = (2, 4, 8, 16)
POOL_GROUP_DIM = 128
POOL_DIM = 512
KV_DIM = 256
N_KEYS = 2 * BLOCK + N_META
D_FF = 2816
RMS_EPS = 1e-6
NEG_INF = -1e30
FFN1_PRE, FFN1_POST, MIX_PRE, MIX_POST, FFN2_PRE, FFN2_POST = range(6)

C_U, C_Q, C_K, C_V, C_GP, C_GA, C_END = 0, 512, 1536, 1792, 2048, 3072, 4096

GATE_CHUNK = 256
POOL_HALO = 16
MIXER_TILE_ROWS = 512
POST_CHUNK = 256
FFN_TILE_ROWS = 1024
FFN_SUB_ROWS = 256
VMEM_LIMIT = 56 * 1024 * 1024
FIRST_CAST_STEPS = 16
BF16_SUBLANES = 16

_BF16 = jnp.bfloat16
_F32 = jnp.float32


def _rmsnorm(x, g):
    ms = jnp.mean(x * x, axis=-1, keepdims=True)
    return x * lax.rsqrt(ms + RMS_EPS) * g


def _sigmoid(x):
    return 1.0 / (1.0 + jnp.exp(-x))


def _dot(a, b):
    return jnp.dot(a, b, preferred_element_type=_F32)


def _dot_tn(a, b):
    return lax.dot_general(a, b, (((0,), (0,)), ((), ())), preferred_element_type=_F32)


def _proj_t(w, x):
    return lax.dot_general(w, x, (((0,), (1,)), ((), ())), preferred_element_type=_F32)


def _const_spec(block_shape, index):
    return pl.BlockSpec(block_shape, lambda i: index, pipeline_mode=pl.Buffered(1))


def _cast_plan(w, index, n_steps):
    rows, cols = w.shape[len(index):]
    n_blocks = max(d for d in range(1, n_steps + 1) if n_steps % d == 0 and rows % (d * BF16_SUBLANES) == 0)
    block_rows, revisit = rows // n_blocks, n_steps // n_blocks
    in_spec = pl.BlockSpec((None,) * len(index) + (block_rows, cols), lambda i: tuple(index) + (i // revisit, 0))
    out_spec = pl.BlockSpec((block_rows, cols), lambda i: (i // revisit, 0))
    return in_spec, out_spec, jax.ShapeDtypeStruct((rows, cols), _BF16)


def _cast_blocks(src_refs, dst_refs):
    for src, dst in zip(src_refs, dst_refs):
        dst[...] = src[...].astype(dst.dtype)


def _cast_kernel(*refs):
    _cast_blocks(refs[:len(refs) // 2], refs[len(refs) // 2:])


def _cast_call(weights, n_steps):
    plans = [_cast_plan(w, index, n_steps) for w, index in weights]
    return pl.pallas_call(
        _cast_kernel,
        grid=(n_steps,),
        in_specs=[p[0] for p in plans],
        out_specs=[p[1] for p in plans],
        out_shape=[p[2] for p in plans],
        compiler_params=pltpu.CompilerParams(dimension_semantics=("arbitrary",), vmem_limit_bytes=VMEM_LIMIT),
        name="cast_bf16",
    )(*[w for w, _ in weights])


def _ffn_rows(h, g_ref, win_ref, wout_ref, pre, post):
    xn = _rmsnorm(h, g_ref[pre:pre + 1, :]).astype(_BF16)
    gate = _dot(xn, win_ref[:, :D_FF])
    up = _dot(xn, win_ref[:, D_FF:])
    act = (gate * _sigmoid(gate) * up).astype(_BF16)
    y = _dot(act, wout_ref[...])
    return h + 0.5 * _rmsnorm(y, g_ref[post:post + 1, :])


def _ffn_kernel(pre, post, n_cast, *refs):
    h_ref, hm_ref, g_ref, win_ref, wout_ref = refs[:5]
    o_ref, om_ref = refs[5 + n_cast:7 + n_cast]
    _cast_blocks(refs[5:5 + n_cast], refs[7 + n_cast:])

    @pl.when(pl.program_id(0) == 0)
    def _():
        om_ref[...] = _ffn_rows(hm_ref[...], g_ref, win_ref, wout_ref, pre, post)

    for r0 in range(0, FFN_TILE_ROWS, FFN_SUB_ROWS):
        o_ref[r0:r0 + FFN_SUB_ROWS, :] = _ffn_rows(
            h_ref[r0:r0 + FFN_SUB_ROWS, :], g_ref, win_ref, wout_ref, pre, post)


def _ffn_call(h, hm, norm_g, w_in, w_out, layer, pre, post, cast_next):
    n_steps = h.shape[0] // FFN_TILE_ROWS
    row_spec = pl.BlockSpec((FFN_TILE_ROWS, D_MODEL), lambda i: (i, 0))
    plans = [_cast_plan(w, index, n_steps) for w, index in cast_next]
    return pl.pallas_call(
        functools.partial(_ffn_kernel, pre, post, len(plans)),
        grid=(n_steps,),
        in_specs=[
            row_spec,
            _const_spec((N_META, D_MODEL), (0, 0)),
            _const_spec((None,) + norm_g.shape[1:], (layer, 0, 0)),
            _const_spec(w_in.shape, (0, 0)),
            _const_spec(w_out.shape, (0, 0)),
        ] + [p[0] for p in plans],
        out_specs=[row_spec, pl.BlockSpec((N_META, D_MODEL), lambda i: (0, 0))] + [p[1] for p in plans],
        out_shape=[jax.ShapeDtypeStruct(h.shape, h.dtype), jax.ShapeDtypeStruct(hm.shape, hm.dtype)]
        + [p[2] for p in plans],
        compiler_params=pltpu.CompilerParams(
            dimension_semantics=("arbitrary",), vmem_limit_bytes=VMEM_LIMIT),
        name="ffn_half_step",
    )(h, hm, norm_g, w_in, w_out, *[w for w, _ in cast_next])


def _alibi_slope(head):
    return 2.0 ** (-8.0 * (head + 1) / N_HEADS)


def _mixer_kernel(layer, n_cast, *refs):
    h_ref, hm_ref, g_ref, win_ref, wgrp_ref, pscale_ref, wpp_ref, sinks_ref, wap_ref, wo_ref = refs[:10]
    o_ref, om_ref = refs[10 + n_cast:12 + n_cast]
    ucarry_ref, kprev_ref, vprev_ref, kmeta_ref, vmeta_ref, bias_ref = refs[12 + 2 * n_cast:]
    _cast_blocks(refs[10:10 + n_cast], refs[12 + n_cast:12 + 2 * n_cast])
    i = pl.program_id(0)
    rd = lax.rem(i, 2)
    wr = 1 - rd

    def mixer_rows(h, is_meta, store):
        rows = h.shape[0]
        n_blocks = rows // BLOCK
        hn = _rmsnorm(h, g_ref[MIX_PRE:MIX_PRE + 1, :]).astype(_BF16)
        row_idx = lax.broadcasted_iota(jnp.int32, (rows, 1), 0)

        k = _dot(hn, win_ref[:, C_K:C_V]).astype(_BF16)
        v_t = _proj_t(win_ref[:, C_V:C_GP], hn).astype(_BF16)
        q_t = (_proj_t(win_ref[:, C_Q:C_K], hn) * (HEAD_DIM ** -0.5)).astype(_BF16)

        if is_meta:
            kmeta_ref[...] = k[:N_META, :]
            vmeta_ref[...] = v_t[:, :N_META]
        else:
            k_meta = kmeta_ref[...]
            v_meta_t = vmeta_ref[...]
            k_carry = kprev_ref[rd]
            v_carry_t = vprev_ref[rd]
            kprev_ref[wr] = k[rows - BLOCK:, :]
            vprev_ref[wr] = v_t[:, rows - BLOCK:]

        def pool_sums():
            u = _dot(hn, win_ref[:, C_U:C_Q])
            if is_meta:
                u = jnp.where(row_idx < N_META, u, 0.0)
                history = jnp.zeros((POOL_HALO, POOL_DIM), _F32)
                ucarry_ref[0] = u[:POOL_HALO, :]
                tok1 = jnp.minimum(row_idx + 1, POOL_WINDOWS[-1])
            else:
                history = ucarry_ref[rd]
                ucarry_ref[wr] = u[rows - POOL_HALO:, :]
            e = jnp.concatenate([history, u], axis=0)
            s2 = e + pltpu.roll(e, 1, 0)
            s4 = s2 + pltpu.roll(s2, 2, 0)
            s8 = s4 + pltpu.roll(s4, 4, 0)
            s16 = s8 + pltpu.roll(s8, 8, 0)
            mixed = []
            for gi, (win, s) in enumerate(zip(POOL_WINDOWS, (s2, s4, s8, s16))):
                lo = gi * POOL_GROUP_DIM
                cnt = jnp.minimum(tok1, win).astype(_F32) if is_meta else float(win)
                pooled = s[POOL_HALO:, lo:lo + POOL_GROUP_DIM] / cnt
                mixed.append((pooled - u[:, lo:lo + POOL_GROUP_DIM]).astype(_BF16))
            return mixed

        def pool_proj(mixed):
            mixed = [_dot(mx, wgrp_ref[gi]) for gi, mx in enumerate(mixed)]
            mixed = (jnp.concatenate(mixed, axis=-1) * pscale_ref[...]).astype(_BF16)
            return _dot(mixed, wpp_ref[...])

        def gate_chunk(c):
            return _sigmoid(_dot(hn, win_ref[:, C_GP + c * GATE_CHUNK:C_GP + (c + 1) * GATE_CHUNK]))

        n_gate_chunks = (C_END - C_GP) // GATE_CHUNK
        side = {}
        side_work = [lambda: side.__setitem__("mixed", pool_sums()),
                     lambda: side.__setitem__("y_pool", pool_proj(side["mixed"]))]
        side_work += [functools.partial(lambda c: side.__setitem__(("gate", c), gate_chunk(c)), c)
                      for c in range(n_gate_chunks)]

        q_idx = lax.broadcasted_iota(jnp.int32, (1, BLOCK), 1)
        if is_meta:
            key_idx = lax.broadcasted_iota(jnp.int32, (BLOCK, 1), 0)
            masks = [((key_idx <= q_idx) & (key_idx < N_META), None)]
        else:
            key_idx = lax.broadcasted_iota(jnp.int32, (N_KEYS, 1), 0)
            first_block_ok = key_idx >= jnp.where(i > 0, 0, BLOCK)
            masks = [(first_block_ok, None)] + [(None, None)] * (n_blocks - 1)
        zeros_q = jnp.zeros((HEAD_DIM, GQA_GROUP * BLOCK), _BF16)

        def scores(b, kh):
            r0 = b * BLOCK
            lanes = slice((kh // 2) * 128, (kh // 2 + 1) * 128)
            if is_meta:
                kk = k[:, lanes]
            else:
                k_prv = k_carry if b == 0 else k[r0 - BLOCK:r0, :]
                kk = jnp.concatenate([k_prv[:, lanes], k[r0:r0 + BLOCK, lanes], k_meta[:, lanes]], axis=0)
            q_stack = jnp.concatenate(
                [q_t[(kh * GQA_GROUP + g) * HEAD_DIM:(kh * GQA_GROUP + g + 1) * HEAD_DIM, r0:r0 + BLOCK]
                 for g in range(GQA_GROUP)], axis=1)
            q_z = jnp.concatenate([q_stack, zeros_q] if kh % 2 == 0 else [zeros_q, q_stack], axis=0)
            return _dot(kk, q_z)

        def attend(b, kh, s_t):
            r0 = b * BLOCK
            ok_b, _ = masks[b]
            rows_v = slice(kh * HEAD_DIM, (kh + 1) * HEAD_DIM)
            if is_meta:
                vv_t = v_t[rows_v, :]
            else:
                v_prv_t = v_carry_t if b == 0 else v_t[:, r0 - BLOCK:r0]
                vv_t = jnp.concatenate(
                    [v_prv_t[rows_v, :], v_t[rows_v, r0:r0 + BLOCK], v_meta_t[rows_v, :]], axis=1)
            probs, inv_l = [], []
            for g in range(GQA_GROUP):
                head = kh * GQA_GROUP + g
                logits = s_t[:, g * BLOCK:(g + 1) * BLOCK]
                if not is_meta:
                    logits = logits + bias_ref[head]
                if ok_b is not None:
                    logits = jnp.where(ok_b, logits, NEG_INF)
                sink = sinks_ref[layer, head]
                m = jnp.maximum(jnp.max(logits, axis=0, keepdims=True), sink)
                p = jnp.exp(logits - m)
                denom = jnp.sum(p, axis=0, keepdims=True) + jnp.exp(sink - m)
                probs.append(p.astype(_BF16))
                inv_l.append(1.0 / denom)
            return vv_t, jnp.concatenate(probs, axis=1), inv_l

        def weighted_values(vv_t, p_t, inv_l):
            o_t = _dot(vv_t, p_t)
            return [o_t[:, g * BLOCK:(g + 1) * BLOCK] * inv_l[g] for g in range(GQA_GROUP)]

        head_rows = {b: [] for b in range(n_blocks)}

        def post_work(b_lo, b_hi):
            r0, r1 = b_lo * BLOCK, b_hi * BLOCK
            st = {}

            def y_piece(c):
                if "attn_t" not in st:
                    st["attn_t"] = jnp.concatenate(
                        [jnp.concatenate(head_rows[b], axis=0) for b in range(b_lo, b_hi)], axis=1).astype(_BF16)
                st["y", c] = _dot_tn(st["attn_t"], wap_ref[:, c * POST_CHUNK:(c + 1) * POST_CHUNK])

            def z_piece(c):
                if "merged" not in st:
                    y_attn = jnp.concatenate([st["y", cc] for cc in range(n_post_chunks)], axis=-1)
                    g_pool = jnp.concatenate([side["gate", cc][r0:r1] for cc in range(n_gate_chunks // 2)], axis=-1)
                    g_attn = jnp.concatenate(
                        [side["gate", cc][r0:r1] for cc in range(n_gate_chunks // 2, n_gate_chunks)], axis=-1)
                    st["merged"] = (g_pool * side["y_pool"][r0:r1] + g_attn * y_attn).astype(_BF16)
                st["z", c] = _dot(st["merged"], wo_ref[:, c * POST_CHUNK:(c + 1) * POST_CHUNK])

            def finish():
                z = jnp.concatenate([st["z", cc] for cc in range(n_post_chunks)], axis=-1)
                store(r0, r1, h[r0:r1] + _rmsnorm(z, g_ref[MIX_POST:MIX_POST + 1, :]))

            n_post_chunks = D_MODEL // POST_CHUNK
            return ([functools.partial(y_piece, c) for c in range(n_post_chunks)]
                    + [functools.partial(z_piece, c) for c in range(n_post_chunks)] + [finish])

        n_groups = 2 if n_blocks % 2 == 0 else 1
        blocks_per_group = n_blocks // n_groups
        steps = [(b, kh) for b in range(n_blocks) for kh in range(N_KV_HEADS)]
        steps_per_group = blocks_per_group * N_KV_HEADS
        queue = list(side_work)
        queue.pop(0)()
        s_next = scores(*steps[0])
        pending = None
        for n, (b, kh) in enumerate(steps):
            s_cur = s_next
            if n + 1 < len(steps):
                s_next = scores(*steps[n + 1])
            soft = attend(b, kh, s_cur)
            if pending is not None:
                head_rows[pending[0]] += weighted_values(*pending[1])
            steps_left = steps_per_group - n % steps_per_group
            pending = (b, soft) if steps_left > 1 else None
            if pending is None:
                head_rows[b] += weighted_values(*soft)
            for _ in range(-(-len(queue) // steps_left)):
                queue.pop(0)()
            if steps_left == 1:
                grp = n // steps_per_group
                queue += post_work(grp * blocks_per_group, (grp + 1) * blocks_per_group)
        while queue:
            queue.pop(0)()

    def store_meta(r0, r1, value):
        om_ref[...] = value[:N_META, :]

    def store_rows(r0, r1, value):
        o_ref[r0:r1, :] = value

    @pl.when(i == 0)
    def _():
        kprev_ref[0] = jnp.zeros((BLOCK, KV_DIM), _BF16)
        vprev_ref[0] = jnp.zeros((KV_DIM, BLOCK), _BF16)
        key_idx = lax.broadcasted_iota(jnp.int32, (N_KEYS, 1), 0)
        q_idx = lax.broadcasted_iota(jnp.int32, (1, BLOCK), 1)
        in_blk = jnp.bitwise_and(key_idx, BLOCK - 1)
        is_meta_key = key_idx >= 2 * BLOCK
        dist = jnp.where(key_idx < BLOCK, q_idx + BLOCK - in_blk, q_idx - in_blk)
        ok = ((dist >= 0) & (dist < WINDOW)) | is_meta_key
        alibi_d = jnp.where(is_meta_key, 0, dist).astype(_F32)
        for head in range(N_HEADS):
            bias_ref[head] = jnp.where(ok, -_alibi_slope(head) * alibi_d, NEG_INF)
        hm = jnp.concatenate([hm_ref[...], jnp.zeros((BLOCK - N_META, D_MODEL), _F32)], axis=0)
        mixer_rows(hm, True, store_meta)

    mixer_rows(h_ref[...], False, store_rows)


def _mixer_call(h, hm, norm_g, w_in, w_grp, pscale, w_pp, sinks, w_ap, w_o, layer, cast_next):
    n_steps = h.shape[0] // MIXER_TILE_ROWS
    row_spec = pl.BlockSpec((MIXER_TILE_ROWS, D_MODEL), lambda i: (i, 0))
    plans = [_cast_plan(w, index, n_steps) for w, index in cast_next]
    return pl.pallas_call(
        functools.partial(_mixer_kernel, layer, len(plans)),
        grid=(n_steps,),
        in_specs=[
            row_spec,
            _const_spec((N_META, D_MODEL), (0, 0)),
            _const_spec((None,) + norm_g.shape[1:], (layer, 0, 0)),
            _const_spec(w_in.shape, (0, 0)),
            _const_spec(w_grp.shape, (0, 0, 0)),
            _const_spec((None,) + pscale.shape[1:], (layer, 0, 0)),
            _const_spec(w_pp.shape, (0, 0)),
            pl.BlockSpec(memory_space=pltpu.SMEM),
            _const_spec(w_ap.shape, (0, 0)),
            _const_spec(w_o.shape, (0, 0)),
        ] + [p[0] for p in plans],
        out_specs=[row_spec, pl.BlockSpec((N_META, D_MODEL), lambda i: (0, 0))] + [p[1] for p in plans],
        out_shape=[jax.ShapeDtypeStruct(h.shape, h.dtype), jax.ShapeDtypeStruct(hm.shape, hm.dtype)]
        + [p[2] for p in plans],
        scratch_shapes=[
            pltpu.VMEM((2, POOL_HALO, POOL_DIM), _F32),
            pltpu.VMEM((2, BLOCK, KV_DIM), _BF16),
            pltpu.VMEM((2, KV_DIM, BLOCK), _BF16),
            pltpu.VMEM((N_META, KV_DIM), _BF16),
            pltpu.VMEM((KV_DIM, N_META), _BF16),
            pltpu.VMEM((N_HEADS, N_KEYS, BLOCK), _F32),
        ],
        compiler_params=pltpu.CompilerParams(
            dimension_semantics=("arbitrary",), vmem_limit_bytes=VMEM_LIMIT),
        name="gated_mixers",
    )(h, hm, norm_g, w_in, w_grp, pscale, w_pp, sinks, w_ap, w_o, *[w for w, _ in cast_next])


def kernel(x, meta_tokens, norm_g, ffn_w_in, ffn_w_out, w_in, pool_w_grp, pool_scale, w_pool_proj, sinks,
           w_attn_proj, w_o):
    batch, seq, d = x.shape
    assert batch == 1 and d == D_MODEL and seq % FFN_TILE_ROWS == 0 and seq % MIXER_TILE_ROWS == 0
    assert meta_tokens.shape == (N_META, D_MODEL) and POOL_HALO == N_META
    depth = norm_g.shape[0]
    n_groups = len(POOL_WINDOWS)
    w_grp_v = pool_w_grp.reshape(depth, n_groups * POOL_GROUP_DIM, POOL_GROUP_DIM)
    pscale = pool_scale.reshape(depth, 1, POOL_DIM)

    def ffn_weights(l, j):
        return [(ffn_w_in, (l, j)), (ffn_w_out, (l, j))]

    def mixer_weights(l):
        return [(w_in, (l,)), (w_grp_v, (l,)), (w_pool_proj, (l,)), (w_attn_proj, (l,)), (w_o, (l,))]

    h, hm = x[0], meta_tokens.astype(x.dtype)
    w_ffn1 = _cast_call(ffn_weights(0, 0), FIRST_CAST_STEPS)
    w_mix = None
    for l in range(depth):
        last = l + 1 == depth
        h, hm, *cast = _ffn_call(h, hm, norm_g, w_ffn1[0], w_ffn1[1], l, FFN1_PRE, FFN1_POST,
                                 mixer_weights(l) if w_mix is None else [])
        w_mix = cast or w_mix
        w_grp = w_mix[1].reshape(n_groups, POOL_GROUP_DIM, POOL_GROUP_DIM)
        ahead = ffn_weights(l, 1) + ([] if last else ffn_weights(l + 1, 0) + mixer_weights(l + 1))
        h, hm, *cast = _mixer_call(h, hm, norm_g, w_mix[0], w_grp, pscale, w_mix[2], sinks, w_mix[3], w_mix[4],
                                   l, ahead)
        h, hm = _ffn_call(h, hm, norm_g, cast[0], cast[1], l, FFN2_PRE, FFN2_POST, [])
        w_ffn1, w_mix = cast[2:4], cast[4:]
    return h[None]
```

```python
import functools

import jax
import jax.numpy as jnp
from jax import lax
from jax.experimental import pallas as pl
from jax.experimental.pallas import tpu as pltpu

D_MODEL = 1024
N_META = 16
N_HEADS = 16
N_KV_HEADS = 4
HEAD_DIM = 64
GQA_GROUP = N_HEADS // N_KV_HEADS
WINDOW = 128
LANES = 128
BLOCK = 128
POOL_WINDOWS = (2, 4, 8, 16)
POOL_GROUP_DIM = 128
POOL_DIM = 512
KV_DIM = 256
N_KEYS = 2 * BLOCK + N_META
D_FF = 2816
RMS_EPS = 1e-6
NEG_INF = -1e30
FFN1_PRE, FFN1_POST, MIX_PRE, MIX_POST, FFN2_PRE, FFN2_POST = range(6)

C_U, C_Q, C_K, C_V, C_GP, C_GA, C_END = 0, 512, 1536, 1792, 2048, 3072, 4096

GATE_CHUNK = 256
POOL_HALO = 16
MIXER_TILE_ROWS = 512
POST_CHUNK = 256
FFN_TILE_ROWS = 1024
FFN_SUB_ROWS = 256
VMEM_LIMIT = 56 * 1024 * 1024
FIRST_CAST_STEPS = 16
BF16_SUBLANES = 16

_BF16 = jnp.bfloat16
_F32 = jnp.float32


def _rmsnorm(x, g):
    ms = jnp.mean(x * x, axis=-1, keepdims=True)
    return x * lax.rsqrt(ms + RMS_EPS) * g


def _sigmoid(x):
    return 1.0 / (1.0 + jnp.exp(-x))


def _dot(a, b):
    return jnp.dot(a, b, preferred_element_type=_F32)


def _dot_tn(a, b):
    return lax.dot_general(a, b, (((0,), (0,)), ((), ())), preferred_element_type=_F32)


def _proj_t(w, x):
    return lax.dot_general(w, x, (((0,), (1,)), ((), ())), preferred_element_type=_F32)


def _const_spec(block_shape, index):
    return pl.BlockSpec(block_shape, lambda i: index, pipeline_mode=pl.Buffered(1))


def _cast_plan(w, index, n_steps):
    rows, cols = w.shape[len(index):]
    n_blocks = max(d for d in range(1, n_steps + 1) if n_steps % d == 0 and rows % (d * BF16_SUBLANES) == 0)
    block_rows, revisit = rows // n_blocks, n_steps // n_blocks
    in_spec = pl.BlockSpec((None,) * len(index) + (block_rows, cols), lambda i: tuple(index) + (i // revisit, 0))
    out_spec = pl.BlockSpec((block_rows, cols), lambda i: (i // revisit, 0))
    return in_spec, out_spec, jax.ShapeDtypeStruct((rows, cols), _BF16)


def _cast_blocks(src_refs, dst_refs):
    for src, dst in zip(src_refs, dst_refs):
        dst[...] = src[...].astype(dst.dtype)


def _cast_kernel(*refs):
    _cast_blocks(refs[:len(refs) // 2], refs[len(refs) // 2:])


def _cast_call(weights, n_steps):
    plans = [_cast_plan(w, index, n_steps) for w, index in weights]
    return pl.pallas_call(
        _cast_kernel,
        grid=(n_steps,),
        in_specs=[p[0] for p in plans],
        out_specs=[p[1] for p in plans],
        out_shape=[p[2] for p in plans],
        compiler_params=pltpu.CompilerParams(dimension_semantics=("arbitrary",), vmem_limit_bytes=VMEM_LIMIT),
        name="cast_bf16",
    )(*[w for w, _ in weights])


def _ffn_rows(h, g_ref, win_ref, wout_ref, pre, post):
    xn = _rmsnorm(h, g_ref[pre:pre + 1, :]).astype(_BF16)
    gate = _dot(xn, win_ref[:, :D_FF])
    up = _dot(xn, win_ref[:, D_FF:])
    act = (gate * _sigmoid(gate) * up).astype(_BF16)
    y = _dot(act, wout_ref[...])
    return h + 0.5 * _rmsnorm(y, g_ref[post:post + 1, :])


def _ffn_kernel(pre, post, n_cast, *refs):
    h_ref, hm_ref, g_ref, win_ref, wout_ref = refs[:5]
    o_ref, om_ref = refs[5 + n_cast:7 + n_cast]
    _cast_blocks(refs[5:5 + n_cast], refs[7 + n_cast:])

    @pl.when(pl.program_id(0) == 0)
    def _():
        om_ref[...] = _ffn_rows(hm_ref[...], g_ref, win_ref, wout_ref, pre, post)

    for r0 in range(0, FFN_TILE_ROWS, FFN_SUB_ROWS):
        o_ref[r0:r0 + FFN_SUB_ROWS, :] = _ffn_rows(
            h_ref[r0:r0 + FFN_SUB_ROWS, :], g_ref, win_ref, wout_ref, pre, post)


def _ffn_call(h, hm, norm_g, w_in, w_out, layer, pre, post, cast_next):
    n_steps = h.shape[0] // FFN_TILE_ROWS
    row_spec = pl.BlockSpec((FFN_TILE_ROWS, D_MODEL), lambda i: (i, 0))
    plans = [_cast_plan(w, index, n_steps) for w, index in cast_next]
    return pl.pallas_call(
        functools.partial(_ffn_kernel, pre, post, len(plans)),
        grid=(n_steps,),
        in_specs=[
            row_spec,
            _const_spec((N_META, D_MODEL), (0, 0)),
            _const_spec((None,) + norm_g.shape[1:], (layer, 0, 0)),
            _const_spec(w_in.shape, (0, 0)),
            _const_spec(w_out.shape, (0, 0)),
        ] + [p[0] for p in plans],
        out_specs=[row_spec, pl.BlockSpec((N_META, D_MODEL), lambda i: (0, 0))] + [p[1] for p in plans],
        out_shape=[jax.ShapeDtypeStruct(h.shape, h.dtype), jax.ShapeDtypeStruct(hm.shape, hm.dtype)]
        + [p[2] for p in plans],
        compiler_params=pltpu.CompilerParams(
            dimension_semantics=("arbitrary",), vmem_limit_bytes=VMEM_LIMIT),
        name="ffn_half_step",
    )(h, hm, norm_g, w_in, w_out, *[w for w, _ in cast_next])


def _alibi_slope(head):
    return 2.0 ** (-8.0 * (head + 1) / N_HEADS)


def _mixer_kernel(layer, n_cast, *refs):
    h_ref, hm_ref, g_ref, win_ref, wgrp_ref, pscale_ref, wpp_ref, sinks_ref, wap_ref, wo_ref = refs[:10]
    o_ref, om_ref = refs[10 + n_cast:12 + n_cast]
    ucarry_ref, kprev_ref, vprev_ref, kmeta_ref, vmeta_ref, bias_ref = refs[12 + 2 * n_cast:]
    _cast_blocks(refs[10:10 + n_cast], refs[12 + n_cast:12 + 2 * n_cast])
    i = pl.program_id(0)
    rd = lax.rem(i, 2)
    wr = 1 - rd

    def mixer_rows(h, is_meta, store):
        rows = h.shape[0]
        n_blocks = rows // BLOCK
        hn = _rmsnorm(h, g_ref[MIX_PRE:MIX_PRE + 1, :]).astype(_BF16)
        row_idx = lax.broadcasted_iota(jnp.int32, (rows, 1), 0)

        k = _dot(hn, win_ref[:, C_K:C_V]).astype(_BF16)
        q_t = (_proj_t(win_ref[:, C_Q:C_K], hn) * (HEAD_DIM ** -0.5)).astype(_BF16)

        if is_meta:
            kmeta_ref[...] = k[:N_META, :]
        else:
            k_meta = kmeta_ref[...]
            k_carry = kprev_ref[rd]
            kprev_ref[wr] = k[rows - BLOCK:, :]

        def pool_sums():
            u = _dot(hn, win_ref[:, C_U:C_Q])
            if is_meta:
                u = jnp.where(row_idx < N_META, u, 0.0)
                history = jnp.zeros((POOL_HALO, POOL_DIM), _F32)
                ucarry_ref[0] = u[:POOL_HALO, :]
                tok1 = jnp.minimum(row_idx + 1, POOL_WINDOWS[-1])
            else:
                history = ucarry_ref[rd]
                ucarry_ref[wr] = u[rows - POOL_HALO:, :]
            e = jnp.concatenate([history, u], axis=0)
            s2 = e + pltpu.roll(e, 1, 0)
            s4 = s2 + pltpu.roll(s2, 2, 0)
            s8 = s4 + pltpu.roll(s4, 4, 0)
            s16 = s8 + pltpu.roll(s8, 8, 0)
            mixed = []
            for gi, (win, s) in enumerate(zip(POOL_WINDOWS, (s2, s4, s8, s16))):
                lo = gi * POOL_GROUP_DIM
                cnt = jnp.minimum(tok1, win).astype(_F32) if is_meta else float(win)
                pooled = s[POOL_HALO:, lo:lo + POOL_GROUP_DIM] / cnt
                mixed.append((pooled - u[:, lo:lo + POOL_GROUP_DIM]).astype(_BF16))
            return mixed

        def pool_proj(mixed):
            mixed = [_dot(mx, wgrp_ref[gi]) for gi, mx in enumerate(mixed)]
            mixed = (jnp.concatenate(mixed, axis=-1) * pscale_ref[...]).astype(_BF16)
            return _dot(mixed, wpp_ref[...])

        def gate_chunk(c):
            return _sigmoid(_dot(hn, win_ref[:, C_GP + c * GATE_CHUNK:C_GP + (c + 1) * GATE_CHUNK]))

        n_gate_chunks = (C_END - C_GP) // GATE_CHUNK
        side = {}
        side_work = [lambda: side.__setitem__("mixed", pool_sums()),
                     lambda: side.__setitem__("y_pool", pool_proj(side["mixed"]))]
        side_work += [functools.partial(lambda c: side.__setitem__(("gate", c), gate_chunk(c)), c)
                      for c in range(n_gate_chunks)]

        q_idx = lax.broadcasted_iota(jnp.int32, (1, BLOCK), 1)
        if is_meta:
            key_idx = lax.broadcasted_iota(jnp.int32, (BLOCK, 1), 0)
            masks = [((key_idx <= q_idx) & (key_idx < N_META), None)]
        else:
            key_idx = lax.broadcasted_iota(jnp.int32, (N_KEYS, 1), 0)
            first_block_ok = key_idx >= jnp.where(i > 0, 0, BLOCK)
            masks = [(first_block_ok, None)] + [(None, None)] * (n_blocks - 1)
        zeros_q = jnp.zeros((HEAD_DIM, GQA_GROUP * BLOCK), _BF16)

        def scores(b, kh):
            r0 = b * BLOCK
            lanes = slice((kh // 2) * LANES, (kh // 2 + 1) * LANES)
            if is_meta:
                kk = k[:, lanes]
            else:
                k_prv = k_carry if b == 0 else k[r0 - BLOCK:r0, :]
                kk = jnp.concatenate([k_prv[:, lanes], k[r0:r0 + BLOCK, lanes], k_meta[:, lanes]], axis=0)
            q_stack = jnp.concatenate(
                [q_t[(kh * GQA_GROUP + g) * HEAD_DIM:(kh * GQA_GROUP + g + 1) * HEAD_DIM, r0:r0 + BLOCK]
                 for g in range(GQA_GROUP)], axis=1)
            q_z = jnp.concatenate([q_stack, zeros_q] if kh % 2 == 0 else [zeros_q, q_stack], axis=0)
            return _dot(kk, q_z)

        def attend(b, kh, s_t):
            r0 = b * BLOCK
            ok_b, _ = masks[b]
            rows_v = slice(kh * HEAD_DIM, (kh + 1) * HEAD_DIM)
            if is_meta:
                vv_t = v_t[rows_v, :]
            else:
                v_prv_t = v_carry_t if b == 0 else v_t[:, r0 - BLOCK:r0]
                vv_t = jnp.concatenate(
                    [v_prv_t[rows_v, :], v_t[rows_v, r0:r0 + BLOCK], v_meta_t[rows_v, :]], axis=1)
            probs, inv_l = [], []
            for g in range(GQA_GROUP):
                head = kh * GQA_GROUP + g
                logits = s_t[:, g * BLOCK:(g + 1) * BLOCK]
                if not is_meta:
                    logits = logits + bias_ref[head]
                if ok_b is not None:
                    logits = jnp.where(ok_b, logits, NEG_INF)
                sink = sinks_ref[layer, head]
                m = jnp.maximum(jnp.max(logits, axis=0, keepdims=True), sink)
                p = jnp.exp(logits - m)
                denom = jnp.sum(p, axis=0, keepdims=True) + jnp.exp(sink - m)
                probs.append(p.astype(_BF16))
                inv_l.append(1.0 / denom)
            return vv_t, jnp.concatenate(probs, axis=1), inv_l

        def weighted_values(vv_t, p_t, inv_l):
            o_t = _dot(vv_t, p_t)
            return [o_t[:, g * BLOCK:(g + 1) * BLOCK] * inv_l[g] for g in range(GQA_GROUP)]

        head_rows = {b: [] for b in range(n_blocks)}

        def post_work(b_lo, b_hi):
            r0, r1 = b_lo * BLOCK, b_hi * BLOCK
            st = {}

            def y_piece(c):
                if "attn_t" not in st:
                    st["attn_t"] = jnp.concatenate(
                        [jnp.concatenate(head_rows[b], axis=0) for b in range(b_lo, b_hi)], axis=1).astype(_BF16)
                st["y", c] = _dot_tn(st["attn_t"], wap_ref[:, c * POST_CHUNK:(c + 1) * POST_CHUNK])

            def z_piece(c):
                if "merged" not in st:
                    y_attn = jnp.concatenate([st["y", cc] for cc in range(n_post_chunks)], axis=-1)
                    g_pool = jnp.concatenate([side["gate", cc][r0:r1] for cc in range(n_gate_chunks // 2)], axis=-1)
                    g_attn = jnp.concatenate(
                        [side["gate", cc][r0:r1] for cc in range(n_gate_chunks // 2, n_gate_chunks)], axis=-1)
                    st["merged"] = (g_pool * side["y_pool"][r0:r1] + g_attn * y_attn).astype(_BF16)
                st["z", c] = _dot(st["merged"], wo_ref[:, c * POST_CHUNK:(c + 1) * POST_CHUNK])

            def finish():
                z = jnp.concatenate([st["z", cc] for cc in range(n_post_chunks)], axis=-1)
                store(r0, r1, h[r0:r1] + _rmsnorm(z, g_ref[MIX_POST:MIX_POST + 1, :]))

            n_post_chunks = D_MODEL // POST_CHUNK
            return ([functools.partial(y_piece, c) for c in range(n_post_chunks)]
                    + [functools.partial(z_piece, c) for c in range(n_post_chunks)] + [finish])

        n_groups = 2 if n_blocks % 2 == 0 else 1
        blocks_per_group = n_blocks // n_groups
        steps = [(b, kh) for b in range(n_blocks) for kh in range(N_KV_HEADS)]
        steps_per_group = blocks_per_group * N_KV_HEADS
        queue = list(side_work)
        queue.pop(0)()
        s_next = scores(*steps[0])
        v_t = _proj_t(win_ref[:, C_V:C_GP], hn).astype(_BF16)
        if is_meta:
            vmeta_ref[...] = v_t[:, :N_META]
        else:
            v_meta_t = vmeta_ref[...]
            v_carry_t = vprev_ref[rd]
            vprev_ref[wr] = v_t[:, rows - BLOCK:]
        pending = None
        for n, (b, kh) in enumerate(steps):
            s_cur = s_next
            if n + 1 < len(steps):
                s_next = scores(*steps[n + 1])
            soft = attend(b, kh, s_cur)
            if pending is not None:
                head_rows[pending[0]] += weighted_values(*pending[1])
            steps_left = steps_per_group - n % steps_per_group
            pending = (b, soft) if steps_left > 1 else None
            if pending is None:
                head_rows[b] += weighted_values(*soft)
            for _ in range(-(-len(queue) // steps_left)):
                queue.pop(0)()
            if steps_left == 1:
                grp = n // steps_per_group
                queue += post_work(grp * blocks_per_group, (grp + 1) * blocks_per_group)
        while queue:
            queue.pop(0)()

    def store_meta(r0, r1, value):
        om_ref[...] = value[:N_META, :]

    def store_rows(r0, r1, value):
        o_ref[r0:r1, :] = value

    @pl.when(i == 0)
    def _():
        kprev_ref[0] = jnp.zeros((BLOCK, KV_DIM), _BF16)
        vprev_ref[0] = jnp.zeros((KV_DIM, BLOCK), _BF16)
        key_idx = lax.broadcasted_iota(jnp.int32, (N_KEYS, 1), 0)
        q_idx = lax.broadcasted_iota(jnp.int32, (1, BLOCK), 1)
        in_blk = jnp.bitwise_and(key_idx, BLOCK - 1)
        is_meta_key = key_idx >= 2 * BLOCK
        dist = jnp.where(key_idx < BLOCK, q_idx + BLOCK - in_blk, q_idx - in_blk)
        ok = ((dist >= 0) & (dist < WINDOW)) | is_meta_key
        alibi_d = jnp.where(is_meta_key, 0, dist).astype(_F32)
        for head in range(N_HEADS):
            bias_ref[head] = jnp.where(ok, -_alibi_slope(head) * alibi_d, NEG_INF)
        hm = jnp.concatenate([hm_ref[...], jnp.zeros((BLOCK - N_META, D_MODEL), _F32)], axis=0)
        mixer_rows(hm, True, store_meta)

    mixer_rows(h_ref[...], False, store_rows)


def _mixer_call(h, hm, norm_g, w_in, w_grp, pscale, w_pp, sinks, w_ap, w_o, layer, cast_next):
    n_steps = h.shape[0] // MIXER_TILE_ROWS
    row_spec = pl.BlockSpec((MIXER_TILE_ROWS, D_MODEL), lambda i: (i, 0))
    plans = [_cast_plan(w, index, n_steps) for w, index in cast_next]
    return pl.pallas_call(
        functools.partial(_mixer_kernel, layer, len(plans)),
        grid=(n_steps,),
        in_specs=[
            row_spec,
            _const_spec((N_META, D_MODEL), (0, 0)),
            _const_spec((None,) + norm_g.shape[1:], (layer, 0, 0)),
            _const_spec(w_in.shape, (0, 0)),
            _const_spec(w_grp.shape, (0, 0, 0)),
            _const_spec((None,) + pscale.shape[1:], (layer, 0, 0)),
            _const_spec(w_pp.shape, (0, 0)),
            pl.BlockSpec(memory_space=pltpu.SMEM),
            _const_spec(w_ap.shape, (0, 0)),
            _const_spec(w_o.shape, (0, 0)),
        ] + [p[0] for p in plans],
        out_specs=[row_spec, pl.BlockSpec((N_META, D_MODEL), lambda i: (0, 0))] + [p[1] for p in plans],
        out_shape=[jax.ShapeDtypeStruct(h.shape, h.dtype), jax.ShapeDtypeStruct(hm.shape, hm.dtype)]
        + [p[2] for p in plans],
        scratch_shapes=[
            pltpu.VMEM((2, POOL_HALO, POOL_DIM), _F32),
            pltpu.VMEM((2, BLOCK, KV_DIM), _BF16),
            pltpu.VMEM((2, KV_DIM, BLOCK), _BF16),
            pltpu.VMEM((N_META, KV_DIM), _BF16),
            pltpu.VMEM((KV_DIM, N_META), _BF16),
            pltpu.VMEM((N_HEADS, N_KEYS, BLOCK), _F32),
        ],
        compiler_params=pltpu.CompilerParams(
            dimension_semantics=("arbitrary",), vmem_limit_bytes=VMEM_LIMIT),
        name="gated_mixers",
    )(h, hm, norm_g, w_in, w_grp, pscale, w_pp, sinks, w_ap, w_o, *[w for w, _ in cast_next])


def kernel(x, meta_tokens, norm_g, ffn_w_in, ffn_w_out, w_in, pool_w_grp, pool_scale, w_pool_proj, sinks,
           w_attn_proj, w_o):
    batch, seq, d = x.shape
    assert batch == 1 and d == D_MODEL and seq % FFN_TILE_ROWS == 0 and seq % MIXER_TILE_ROWS == 0
    assert meta_tokens.shape == (N_META, D_MODEL) and POOL_HALO == N_META
    depth = norm_g.shape[0]
    n_groups = len(POOL_WINDOWS)
    w_grp_v = pool_w_grp.reshape(depth, n_groups * POOL_GROUP_DIM, POOL_GROUP_DIM)
    pscale = pool_scale.reshape(depth, 1, POOL_DIM)

    def ffn_weights(l, j):
        return [(ffn_w_in, (l, j)), (ffn_w_out, (l, j))]

    def mixer_weights(l):
        return [(w_in, (l,)), (w_grp_v, (l,)), (w_pool_proj, (l,)), (w_attn_proj, (l,)), (w_o, (l,))]

    h, hm = x[0], meta_tokens.astype(x.dtype)
    w_ffn1 = _cast_call(ffn_weights(0, 0), FIRST_CAST_STEPS)
    w_mix = None
    for l in range(depth):
        last = l + 1 == depth
        h, hm, *cast = _ffn_call(h, hm, norm_g, w_ffn1[0], w_ffn1[1], l, FFN1_PRE, FFN1_POST,
                                 mixer_weights(l) if w_mix is None else [])
        w_mix = cast or w_mix
        w_grp = w_mix[1].reshape(n_groups, POOL_GROUP_DIM, POOL_GROUP_DIM)
        ahead = ffn_weights(l, 1) + ([] if last else ffn_weights(l + 1, 0) + mixer_weights(l + 1))
        h, hm, *cast = _mixer_call(h, hm, norm_g, w_mix[0], w_grp, pscale, w_mix[2], sinks, w_mix[3], w_mix[4],
                                   l, ahead)
        h, hm = _ffn_call(h, hm, norm_g, cast[0], cast[1], l, FFN2_PRE, FFN2_POST, [])
        w_ffn1, w_mix = cast[2:4], cast[4:]
    return h[None]
```

```python
import functools

import jax
import jax.numpy as jnp
from jax import lax
from jax.experimental import pallas as pl
from jax.experimental.pallas import tpu as pltpu

D_MODEL = 1024
N_META = 16
N_HEADS = 16
N_KV_HEADS = 4
HEAD_DIM = 64
GQA_GROUP = N_HEADS // N_KV_HEADS
WINDOW = 128
LANES = 128
BLOCK = 128
POOL_WINDOWS = (2, 4, 8, 16)
POOL_GROUP_DIM = 128
POOL_DIM = 512
KV_DIM = 256
N_KEYS = 2 * BLOCK + N_META
D_FF = 2816
RMS_EPS = 1e-6
NEG_INF = -1e30
LOG2_E = 1.4426950408889634
FFN1_PRE, FFN1_POST, MIX_PRE, MIX_POST, FFN2_PRE, FFN2_POST = range(6)

C_U, C_Q, C_K, C_V, C_GP, C_GA, C_END = 0, 512, 1536, 1792, 2048, 3072, 4096

GATE_CHUNK = 256
POOL_HALO = 16
MIXER_TILE_ROWS = 512
POST_CHUNK = 256
FFN_TILE_ROWS = 1024
FFN_SUB_ROWS = 256
VMEM_LIMIT = 56 * 1024 * 1024
FIRST_CAST_STEPS = 16
BF16_SUBLANES = 16

_BF16 = jnp.bfloat16
_F32 = jnp.float32


def _rmsnorm(x, g):
    ms = jnp.mean(x * x, axis=-1, keepdims=True)
    return x * lax.rsqrt(ms + RMS_EPS) * g


def _sigmoid(x):
    return 1.0 / (1.0 + jnp.exp(-x))


def _dot(a, b):
    return jnp.dot(a, b, preferred_element_type=_F32)


def _dot_tn(a, b):
    return lax.dot_general(a, b, (((0,), (0,)), ((), ())), preferred_element_type=_F32)


def _proj_t(w, x):
    return lax.dot_general(w, x, (((0,), (1,)), ((), ())), preferred_element_type=_F32)


def _const_spec(block_shape, index):
    return pl.BlockSpec(block_shape, lambda i: index, pipeline_mode=pl.Buffered(1))


def _cast_plan(w, index, n_steps):
    rows, cols = w.shape[len(index):]
    n_blocks = max(d for d in range(1, n_steps + 1) if n_steps % d == 0 and rows % (d * BF16_SUBLANES) == 0)
    block_rows, revisit = rows // n_blocks, n_steps // n_blocks
    in_spec = pl.BlockSpec((None,) * len(index) + (block_rows, cols), lambda i: tuple(index) + (i // revisit, 0))
    out_spec = pl.BlockSpec((block_rows, cols), lambda i: (i // revisit, 0))
    return in_spec, out_spec, jax.ShapeDtypeStruct((rows, cols), _BF16)


def _cast_blocks(src_refs, dst_refs):
    for src, dst in zip(src_refs, dst_refs):
        dst[...] = src[...].astype(dst.dtype)


def _cast_kernel(*refs):
    _cast_blocks(refs[:len(refs) // 2], refs[len(refs) // 2:])


def _cast_call(weights, n_steps):
    plans = [_cast_plan(w, index, n_steps) for w, index in weights]
    return pl.pallas_call(
        _cast_kernel,
        grid=(n_steps,),
        in_specs=[p[0] for p in plans],
        out_specs=[p[1] for p in plans],
        out_shape=[p[2] for p in plans],
        compiler_params=pltpu.CompilerParams(dimension_semantics=("arbitrary",), vmem_limit_bytes=VMEM_LIMIT),
        name="cast_bf16",
    )(*[w for w, _ in weights])


def _ffn_rows(h, g_ref, win_ref, wout_ref, pre, post):
    xn = _rmsnorm(h, g_ref[pre:pre + 1, :]).astype(_BF16)
    gate = _dot(xn, win_ref[:, :D_FF])
    up = _dot(xn, win_ref[:, D_FF:])
    act = (gate * _sigmoid(gate) * up).astype(_BF16)
    y = _dot(act, wout_ref[...])
    return h + 0.5 * _rmsnorm(y, g_ref[post:post + 1, :])


def _ffn_kernel(pre, post, n_cast, *refs):
    h_ref, hm_ref, g_ref, win_ref, wout_ref = refs[:5]
    o_ref, om_ref = refs[5 + n_cast:7 + n_cast]
    _cast_blocks(refs[5:5 + n_cast], refs[7 + n_cast:])

    @pl.when(pl.program_id(0) == 0)
    def _():
        om_ref[...] = _ffn_rows(hm_ref[...], g_ref, win_ref, wout_ref, pre, post)

    for r0 in range(0, FFN_TILE_ROWS, FFN_SUB_ROWS):
        o_ref[r0:r0 + FFN_SUB_ROWS, :] = _ffn_rows(
            h_ref[r0:r0 + FFN_SUB_ROWS, :], g_ref, win_ref, wout_ref, pre, post)


def _ffn_call(h, hm, norm_g, w_in, w_out, layer, pre, post, cast_next):
    n_steps = h.shape[0] // FFN_TILE_ROWS
    row_spec = pl.BlockSpec((FFN_TILE_ROWS, D_MODEL), lambda i: (i, 0))
    plans = [_cast_plan(w, index, n_steps) for w, index in cast_next]
    return pl.pallas_call(
        functools.partial(_ffn_kernel, pre, post, len(plans)),
        grid=(n_steps,),
        in_specs=[
            row_spec,
            _const_spec((N_META, D_MODEL), (0, 0)),
            _const_spec((None,) + norm_g.shape[1:], (layer, 0, 0)),
            _const_spec(w_in.shape, (0, 0)),
            _const_spec(w_out.shape, (0, 0)),
        ] + [p[0] for p in plans],
        out_specs=[row_spec, pl.BlockSpec((N_META, D_MODEL), lambda i: (0, 0))] + [p[1] for p in plans],
        out_shape=[jax.ShapeDtypeStruct(h.shape, h.dtype), jax.ShapeDtypeStruct(hm.shape, hm.dtype)]
        + [p[2] for p in plans],
        compiler_params=pltpu.CompilerParams(
            dimension_semantics=("arbitrary",), vmem_limit_bytes=VMEM_LIMIT),
        name="ffn_half_step",
    )(h, hm, norm_g, w_in, w_out, *[w for w, _ in cast_next])


def _alibi_slope(head):
    return 2.0 ** (-8.0 * (head + 1) / N_HEADS)


def _mixer_kernel(layer, n_cast, *refs):
    h_ref, hm_ref, g_ref, win_ref, wgrp_ref, pscale_ref, wpp_ref, sinks_ref, wap_ref, wo_ref = refs[:10]
    o_ref, om_ref = refs[10 + n_cast:12 + n_cast]
    ucarry_ref, kprev_ref, vprev_ref, kmeta_ref, vmeta_ref, bias_ref = refs[12 + 2 * n_cast:]
    _cast_blocks(refs[10:10 + n_cast], refs[12 + n_cast:12 + 2 * n_cast])
    i = pl.program_id(0)
    rd = lax.rem(i, 2)
    wr = 1 - rd

    def mixer_rows(h, is_meta, store):
        rows = h.shape[0]
        n_blocks = rows // BLOCK
        hn = _rmsnorm(h, g_ref[MIX_PRE:MIX_PRE + 1, :]).astype(_BF16)
        row_idx = lax.broadcasted_iota(jnp.int32, (rows, 1), 0)

        k = _dot(hn, win_ref[:, C_K:C_V]).astype(_BF16)
        q_t = (_proj_t(win_ref[:, C_Q:C_K], hn) * (HEAD_DIM ** -0.5 * LOG2_E)).astype(_BF16)

        if is_meta:
            kmeta_ref[...] = k[:N_META, :]
        else:
            k_meta = kmeta_ref[...]
            k_carry = kprev_ref[rd]
            kprev_ref[wr] = k[rows - BLOCK:, :]

        def pool_sums():
            u = _dot(hn, win_ref[:, C_U:C_Q])
            if is_meta:
                u = jnp.where(row_idx < N_META, u, 0.0)
                history = jnp.zeros((POOL_HALO, POOL_DIM), _F32)
                ucarry_ref[0] = u[:POOL_HALO, :]
                tok1 = jnp.minimum(row_idx + 1, POOL_WINDOWS[-1])
            else:
                history = ucarry_ref[rd]
                ucarry_ref[wr] = u[rows - POOL_HALO:, :]
            e = jnp.concatenate([history, u], axis=0)
            s2 = e + pltpu.roll(e, 1, 0)
            s4 = s2 + pltpu.roll(s2, 2, 0)
            s8 = s4 + pltpu.roll(s4, 4, 0)
            s16 = s8 + pltpu.roll(s8, 8, 0)
            mixed = []
            for gi, (win, s) in enumerate(zip(POOL_WINDOWS, (s2, s4, s8, s16))):
                lo = gi * POOL_GROUP_DIM
                cnt = jnp.minimum(tok1, win).astype(_F32) if is_meta else float(win)
                pooled = s[POOL_HALO:, lo:lo + POOL_GROUP_DIM] / cnt
                mixed.append((pooled - u[:, lo:lo + POOL_GROUP_DIM]).astype(_BF16))
            return mixed

        def pool_proj(mixed):
            mixed = [_dot(mx, wgrp_ref[gi]) for gi, mx in enumerate(mixed)]
            mixed = (jnp.concatenate(mixed, axis=-1) * pscale_ref[...]).astype(_BF16)
            return _dot(mixed, wpp_ref[...])

        def gate_chunk(c):
            return _sigmoid(_dot(hn, win_ref[:, C_GP + c * GATE_CHUNK:C_GP + (c + 1) * GATE_CHUNK]))

        n_gate_chunks = (C_END - C_GP) // GATE_CHUNK
        side = {}
        side_work = [lambda: side.__setitem__("mixed", pool_sums()),
                     lambda: side.__setitem__("y_pool", pool_proj(side["mixed"]))]
        side_work += [functools.partial(lambda c: side.__setitem__(("gate", c), gate_chunk(c)), c)
                      for c in range(n_gate_chunks)]

        q_idx = lax.broadcasted_iota(jnp.int32, (1, BLOCK), 1)
        if is_meta:
            key_idx = lax.broadcasted_iota(jnp.int32, (BLOCK, 1), 0)
            masks = [((key_idx <= q_idx) & (key_idx < N_META), None)]
        else:
            key_idx = lax.broadcasted_iota(jnp.int32, (N_KEYS, 1), 0)
            first_block_ok = key_idx >= jnp.where(i > 0, 0, BLOCK)
            masks = [(first_block_ok, None)] + [(None, None)] * (n_blocks - 1)
        zeros_q = jnp.zeros((HEAD_DIM, GQA_GROUP * BLOCK), _BF16)

        def scores(b, kh):
            r0 = b * BLOCK
            lanes = slice((kh // 2) * LANES, (kh // 2 + 1) * LANES)
            if is_meta:
                kk = k[:, lanes]
            else:
                k_prv = k_carry if b == 0 else k[r0 - BLOCK:r0, :]
                kk = jnp.concatenate([k_prv[:, lanes], k[r0:r0 + BLOCK, lanes], k_meta[:, lanes]], axis=0)
            q_stack = jnp.concatenate(
                [q_t[(kh * GQA_GROUP + g) * HEAD_DIM:(kh * GQA_GROUP + g + 1) * HEAD_DIM, r0:r0 + BLOCK]
                 for g in range(GQA_GROUP)], axis=1)
            q_z = jnp.concatenate([q_stack, zeros_q] if kh % 2 == 0 else [zeros_q, q_stack], axis=0)
            return _dot(kk, q_z)

        def attend(b, kh, s_t):
            r0 = b * BLOCK
            ok_b, _ = masks[b]
            rows_v = slice(kh * HEAD_DIM, (kh + 1) * HEAD_DIM)
            if is_meta:
                vv_t = v_t[rows_v, :]
            else:
                v_prv_t = v_carry_t if b == 0 else v_t[:, r0 - BLOCK:r0]
                vv_t = jnp.concatenate(
                    [v_prv_t[rows_v, :], v_t[rows_v, r0:r0 + BLOCK], v_meta_t[rows_v, :]], axis=1)
            probs, inv_l = [], []
            for g in range(GQA_GROUP):
                head = kh * GQA_GROUP + g
                logits = s_t[:, g * BLOCK:(g + 1) * BLOCK]
                if not is_meta:
                    logits = logits + bias_ref[head]
                if ok_b is not None:
                    logits = jnp.where(ok_b, logits, NEG_INF)
                sink = sinks_ref[layer, head] * LOG2_E
                m = jnp.maximum(jnp.max(logits, axis=0, keepdims=True), sink)
                p = jnp.exp2(logits - m)
                denom = jnp.sum(p, axis=0, keepdims=True) + jnp.exp2(sink - m)
                probs.append(p.astype(_BF16))
                inv_l.append(1.0 / denom)
            return vv_t, jnp.concatenate(probs, axis=1), inv_l

        def weighted_values(vv_t, p_t, inv_l):
            o_t = _dot(vv_t, p_t)
            return [o_t[:, g * BLOCK:(g + 1) * BLOCK] * inv_l[g] for g in range(GQA_GROUP)]

        head_rows = {b: [] for b in range(n_blocks)}

        def post_work(b_lo, b_hi):
            r0, r1 = b_lo * BLOCK, b_hi * BLOCK
            st = {}

            def y_piece(c):
                if "attn_t" not in st:
                    st["attn_t"] = jnp.concatenate(
                        [jnp.concatenate(head_rows[b], axis=0) for b in range(b_lo, b_hi)], axis=1).astype(_BF16)
                st["y", c] = _dot_tn(st["attn_t"], wap_ref[:, c * POST_CHUNK:(c + 1) * POST_CHUNK])

            def z_piece(c):
                if "merged" not in st:
                    y_attn = jnp.concatenate([st["y", cc] for cc in range(n_post_chunks)], axis=-1)
                    g_pool = jnp.concatenate([side["gate", cc][r0:r1] for cc in range(n_gate_chunks // 2)], axis=-1)
                    g_attn = jnp.concatenate(
                        [side["gate", cc][r0:r1] for cc in range(n_gate_chunks // 2, n_gate_chunks)], axis=-1)
                    st["merged"] = (g_pool * side["y_pool"][r0:r1] + g_attn * y_attn).astype(_BF16)
                st["z", c] = _dot(st["merged"], wo_ref[:, c * POST_CHUNK:(c + 1) * POST_CHUNK])

            def finish():
                z = jnp.concatenate([st["z", cc] for cc in range(n_post_chunks)], axis=-1)
                store(r0, r1, h[r0:r1] + _rmsnorm(z, g_ref[MIX_POST:MIX_POST + 1, :]))

            n_post_chunks = D_MODEL // POST_CHUNK
            return ([functools.partial(y_piece, c) for c in range(n_post_chunks)]
                    + [functools.partial(z_piece, c) for c in range(n_post_chunks)] + [finish])

        n_groups = 2 if n_blocks % 2 == 0 else 1
        blocks_per_group = n_blocks // n_groups
        steps = [(b, kh) for b in range(n_blocks) for kh in range(N_KV_HEADS)]
        steps_per_group = blocks_per_group * N_KV_HEADS
        queue = list(side_work)
        queue.pop(0)()
        s_next = scores(*steps[0])
        v_t = _proj_t(win_ref[:, C_V:C_GP], hn).astype(_BF16)
        if is_meta:
            vmeta_ref[...] = v_t[:, :N_META]
        else:
            v_meta_t = vmeta_ref[...]
            v_carry_t = vprev_ref[rd]
            vprev_ref[wr] = v_t[:, rows - BLOCK:]
        pending = None
        for n, (b, kh) in enumerate(steps):
            s_cur = s_next
            if n + 1 < len(steps):
                s_next = scores(*steps[n + 1])
            soft = attend(b, kh, s_cur)
            if pending is not None:
                head_rows[pending[0]] += weighted_values(*pending[1])
            steps_left = steps_per_group - n % steps_per_group
            pending = (b, soft) if steps_left > 1 else None
            if pending is None:
                head_rows[b] += weighted_values(*soft)
            for _ in range(-(-len(queue) // steps_left)):
                queue.pop(0)()
            if steps_left == 1:
                grp = n // steps_per_group
                queue += post_work(grp * blocks_per_group, (grp + 1) * blocks_per_group)
        while queue:
            queue.pop(0)()

    def store_meta(r0, r1, value):
        om_ref[...] = value[:N_META, :]

    def store_rows(r0, r1, value):
        o_ref[r0:r1, :] = value

    @pl.when(i == 0)
    def _():
        kprev_ref[0] = jnp.zeros((BLOCK, KV_DIM), _BF16)
        vprev_ref[0] = jnp.zeros((KV_DIM, BLOCK), _BF16)
        key_idx = lax.broadcasted_iota(jnp.int32, (N_KEYS, 1), 0)
        q_idx = lax.broadcasted_iota(jnp.int32, (1, BLOCK), 1)
        in_blk = jnp.bitwise_and(key_idx, BLOCK - 1)
        is_meta_key = key_idx >= 2 * BLOCK
        dist = jnp.where(key_idx < BLOCK, q_idx + BLOCK - in_blk, q_idx - in_blk)
        ok = ((dist >= 0) & (dist < WINDOW)) | is_meta_key
        alibi_d = jnp.where(is_meta_key, 0, dist).astype(_F32)
        for head in range(N_HEADS):
            bias_ref[head] = jnp.where(ok, (-_alibi_slope(head) * LOG2_E) * alibi_d, NEG_INF)
        hm = jnp.concatenate([hm_ref[...], jnp.zeros((BLOCK - N_META, D_MODEL), _F32)], axis=0)
        mixer_rows(hm, True, store_meta)

    mixer_rows(h_ref[...], False, store_rows)


def _mixer_call(h, hm, norm_g, w_in, w_grp, pscale, w_pp, sinks, w_ap, w_o, layer, cast_next):
    n_steps = h.shape[0] // MIXER_TILE_ROWS
    row_spec = pl.BlockSpec((MIXER_TILE_ROWS, D_MODEL), lambda i: (i, 0))
    plans = [_cast_plan(w, index, n_steps) for w, index in cast_next]
    return pl.pallas_call(
        functools.partial(_mixer_kernel, layer, len(plans)),
        grid=(n_steps,),
        in_specs=[
            row_spec,
            _const_spec((N_META, D_MODEL), (0, 0)),
            _const_spec((None,) + norm_g.shape[1:], (layer, 0, 0)),
            _const_spec(w_in.shape, (0, 0)),
            _const_spec(w_grp.shape, (0, 0, 0)),
            _const_spec((None,) + pscale.shape[1:], (layer, 0, 0)),
            _const_spec(w_pp.shape, (0, 0)),
            pl.BlockSpec(memory_space=pltpu.SMEM),
            _const_spec(w_ap.shape, (0, 0)),
            _const_spec(w_o.shape, (0, 0)),
        ] + [p[0] for p in plans],
        out_specs=[row_spec, pl.BlockSpec((N_META, D_MODEL), lambda i: (0, 0))] + [p[1] for p in plans],
        out_shape=[jax.ShapeDtypeStruct(h.shape, h.dtype), jax.ShapeDtypeStruct(hm.shape, hm.dtype)]
        + [p[2] for p in plans],
        scratch_shapes=[
            pltpu.VMEM((2, POOL_HALO, POOL_DIM), _F32),
            pltpu.VMEM((2, BLOCK, KV_DIM), _BF16),
            pltpu.VMEM((2, KV_DIM, BLOCK), _BF16),
            pltpu.VMEM((N_META, KV_DIM), _BF16),
            pltpu.VMEM((KV_DIM, N_META), _BF16),
            pltpu.VMEM((N_HEADS, N_KEYS, BLOCK), _F32),
        ],
        compiler_params=pltpu.CompilerParams(
            dimension_semantics=("arbitrary",), vmem_limit_bytes=VMEM_LIMIT),
        name="gated_mixers",
    )(h, hm, norm_g, w_in, w_grp, pscale, w_pp, sinks, w_ap, w_o, *[w for w, _ in cast_next])


def kernel(x, meta_tokens, norm_g, ffn_w_in, ffn_w_out, w_in, pool_w_grp, pool_scale, w_pool_proj, sinks,
           w_attn_proj, w_o):
    batch, seq, d = x.shape
    assert batch == 1 and d == D_MODEL and seq % FFN_TILE_ROWS == 0 and seq % MIXER_TILE_ROWS == 0
    assert meta_tokens.shape == (N_META, D_MODEL) and POOL_HALO == N_META
    depth = norm_g.shape[0]
    n_groups = len(POOL_WINDOWS)
    w_grp_v = pool_w_grp.reshape(depth, n_groups * POOL_GROUP_DIM, POOL_GROUP_DIM)
    pscale = pool_scale.reshape(depth, 1, POOL_DIM)

    def ffn_weights(l, j):
        return [(ffn_w_in, (l, j)), (ffn_w_out, (l, j))]

    def mixer_weights(l):
        return [(w_in, (l,)), (w_grp_v, (l,)), (w_pool_proj, (l,)), (w_attn_proj, (l,)), (w_o, (l,))]

    h, hm = x[0], meta_tokens.astype(x.dtype)
    w_ffn1 = _cast_call(ffn_weights(0, 0), FIRST_CAST_STEPS)
    w_mix = None
    for l in range(depth):
        last = l + 1 == depth
        h, hm, *cast = _ffn_call(h, hm, norm_g, w_ffn1[0], w_ffn1[1], l, FFN1_PRE, FFN1_POST,
                                 mixer_weights(l) if w_mix is None else [])
        w_mix = cast or w_mix
        w_grp = w_mix[1].reshape(n_groups, POOL_GROUP_DIM, POOL_GROUP_DIM)
        ahead = ffn_weights(l, 1) + ([] if last else ffn_weights(l + 1, 0) + mixer_weights(l + 1))
        h, hm, *cast = _mixer_call(h, hm, norm_g, w_mix[0], w_grp, pscale, w_mix[2], sinks, w_mix[3], w_mix[4],
                                   l, ahead)
        h, hm = _ffn_call(h, hm, norm_g, cast[0], cast[1], l, FFN2_PRE, FFN2_POST, [])
        w_ffn1, w_mix = cast[2:4], cast[4:]
    return h[None]
```

```python
import functools

import jax
import jax.numpy as jnp
from jax import lax
from jax.experimental import pallas as pl
from jax.experimental.pallas import tpu as pltpu

D_MODEL = 1024
N_META = 16
N_HEADS = 16
N_KV_HEADS = 4
HEAD_DIM = 64
GQA_GROUP = N_HEADS // N_KV_HEADS
WINDOW = 128
LANES = 128
BLOCK = 128
POOL_WINDOWS = (2, 4, 8, 16)
POOL_GROUP_DIM = 128
POOL_DIM = 512
KV_DIM = 256
N_KEYS = 2 * BLOCK + N_META
D_FF = 2816
RMS_EPS = 1e-6
NEG_INF = -1e30
LOG2_E = 1.4426950408889634
FFN1_PRE, FFN1_POST, MIX_PRE, MIX_POST, FFN2_PRE, FFN2_POST = range(6)

C_U, C_Q, C_K, C_V, C_GP, C_GA, C_END = 0, 512, 1536, 1792, 2048, 3072, 4096

GATE_CHUNK = 256
POOL_HALO = 16
MIXER_TILE_ROWS = 512
POST_CHUNK = 256
FFN_TILE_ROWS = 1024
FFN_SUB_ROWS = 256
VMEM_LIMIT = 56 * 1024 * 1024
FIRST_CAST_STEPS = 16
BF16_SUBLANES = 16

_BF16 = jnp.bfloat16
_F32 = jnp.float32


def _rmsnorm(x, g):
    ms = jnp.mean(x * x, axis=-1, keepdims=True)
    return x * lax.rsqrt(ms + RMS_EPS) * g


def _sigmoid(x):
    return 0.5 * jnp.tanh(0.5 * x) + 0.5


def _dot(a, b):
    return jnp.dot(a, b, preferred_element_type=_F32)


def _dot_tn(a, b):
    return lax.dot_general(a, b, (((0,), (0,)), ((), ())), preferred_element_type=_F32)


def _proj_t(w, x):
    return lax.dot_general(w, x, (((0,), (1,)), ((), ())), preferred_element_type=_F32)


def _const_spec(block_shape, index):
    return pl.BlockSpec(block_shape, lambda i: index, pipeline_mode=pl.Buffered(1))


def _cast_plan(w, index, n_steps):
    rows, cols = w.shape[len(index):]
    n_blocks = max(d for d in range(1, n_steps + 1) if n_steps % d == 0 and rows % (d * BF16_SUBLANES) == 0)
    block_rows, revisit = rows // n_blocks, n_steps // n_blocks
    in_spec = pl.BlockSpec((None,) * len(index) + (block_rows, cols), lambda i: tuple(index) + (i // revisit, 0))
    out_spec = pl.BlockSpec((block_rows, cols), lambda i: (i // revisit, 0))
    return in_spec, out_spec, jax.ShapeDtypeStruct((rows, cols), _BF16)


def _cast_blocks(src_refs, dst_refs):
    for src, dst in zip(src_refs, dst_refs):
        dst[...] = src[...].astype(dst.dtype)


def _cast_kernel(*refs):
    _cast_blocks(refs[:len(refs) // 2], refs[len(refs) // 2:])


def _cast_call(weights, n_steps):
    plans = [_cast_plan(w, index, n_steps) for w, index in weights]
    return pl.pallas_call(
        _cast_kernel,
        grid=(n_steps,),
        in_specs=[p[0] for p in plans],
        out_specs=[p[1] for p in plans],
        out_shape=[p[2] for p in plans],
        compiler_params=pltpu.CompilerParams(dimension_semantics=("arbitrary",), vmem_limit_bytes=VMEM_LIMIT),
        name="cast_bf16",
    )(*[w for w, _ in weights])


def _ffn_rows(h, g_ref, win_ref, wout_ref, pre, post):
    xn = _rmsnorm(h, g_ref[pre:pre + 1, :]).astype(_BF16)
    gate = _dot(xn, win_ref[:, :D_FF])
    up = _dot(xn, win_ref[:, D_FF:])
    act = (gate * _sigmoid(gate) * up).astype(_BF16)
    y = _dot(act, wout_ref[...])
    return h + 0.5 * _rmsnorm(y, g_ref[post:post + 1, :])


def _ffn_kernel(pre, post, n_cast, *refs):
    h_ref, hm_ref, g_ref, win_ref, wout_ref = refs[:5]
    o_ref, om_ref = refs[5 + n_cast:7 + n_cast]
    _cast_blocks(refs[5:5 + n_cast], refs[7 + n_cast:])

    @pl.when(pl.program_id(0) == 0)
    def _():
        om_ref[...] = _ffn_rows(hm_ref[...], g_ref, win_ref, wout_ref, pre, post)

    for r0 in range(0, FFN_TILE_ROWS, FFN_SUB_ROWS):
        o_ref[r0:r0 + FFN_SUB_ROWS, :] = _ffn_rows(
            h_ref[r0:r0 + FFN_SUB_ROWS, :], g_ref, win_ref, wout_ref, pre, post)


def _ffn_call(h, hm, norm_g, w_in, w_out, layer, pre, post, cast_next):
    n_steps = h.shape[0] // FFN_TILE_ROWS
    row_spec = pl.BlockSpec((FFN_TILE_ROWS, D_MODEL), lambda i: (i, 0))
    plans = [_cast_plan(w, index, n_steps) for w, index in cast_next]
    return pl.pallas_call(
        functools.partial(_ffn_kernel, pre, post, len(plans)),
        grid=(n_steps,),
        in_specs=[
            row_spec,
            _const_spec((N_META, D_MODEL), (0, 0)),
            _const_spec((None,) + norm_g.shape[1:], (layer, 0, 0)),
            _const_spec(w_in.shape, (0, 0)),
            _const_spec(w_out.shape, (0, 0)),
        ] + [p[0] for p in plans],
        out_specs=[row_spec, pl.BlockSpec((N_META, D_MODEL), lambda i: (0, 0))] + [p[1] for p in plans],
        out_shape=[jax.ShapeDtypeStruct(h.shape, h.dtype), jax.ShapeDtypeStruct(hm.shape, hm.dtype)]
        + [p[2] for p in plans],
        compiler_params=pltpu.CompilerParams(
            dimension_semantics=("arbitrary",), vmem_limit_bytes=VMEM_LIMIT),
        name="ffn_half_step",
    )(h, hm, norm_g, w_in, w_out, *[w for w, _ in cast_next])


def _alibi_slope(head):
    return 2.0 ** (-8.0 * (head + 1) / N_HEADS)


def _mixer_kernel(layer, n_cast, *refs):
    h_ref, hm_ref, g_ref, win_ref, wgrp_ref, pscale_ref, wpp_ref, sinks_ref, wap_ref, wo_ref = refs[:10]
    o_ref, om_ref = refs[10 + n_cast:12 + n_cast]
    ucarry_ref, kprev_ref, vprev_ref, kmeta_ref, vmeta_ref, bias_ref = refs[12 + 2 * n_cast:]
    _cast_blocks(refs[10:10 + n_cast], refs[12 + n_cast:12 + 2 * n_cast])
    i = pl.program_id(0)
    rd = lax.rem(i, 2)
    wr = 1 - rd

    def mixer_rows(h, is_meta, store):
        rows = h.shape[0]
        n_blocks = rows // BLOCK
        hn = _rmsnorm(h, g_ref[MIX_PRE:MIX_PRE + 1, :]).astype(_BF16)
        row_idx = lax.broadcasted_iota(jnp.int32, (rows, 1), 0)

        k = _dot(hn, win_ref[:, C_K:C_V]).astype(_BF16)
        q_t = (_proj_t(win_ref[:, C_Q:C_K], hn) * (HEAD_DIM ** -0.5 * LOG2_E)).astype(_BF16)

        if is_meta:
            kmeta_ref[...] = k[:N_META, :]
        else:
            k_meta = kmeta_ref[...]
            k_carry = kprev_ref[rd]
            kprev_ref[wr] = k[rows - BLOCK:, :]

        def pool_sums():
            u = _dot(hn, win_ref[:, C_U:C_Q])
            if is_meta:
                u = jnp.where(row_idx < N_META, u, 0.0)
                history = jnp.zeros((POOL_HALO, POOL_DIM), _F32)
                ucarry_ref[0] = u[:POOL_HALO, :]
                tok1 = jnp.minimum(row_idx + 1, POOL_WINDOWS[-1])
            else:
                history = ucarry_ref[rd]
                ucarry_ref[wr] = u[rows - POOL_HALO:, :]
            e = jnp.concatenate([history, u], axis=0)
            s2 = e + pltpu.roll(e, 1, 0)
            s4 = s2 + pltpu.roll(s2, 2, 0)
            s8 = s4 + pltpu.roll(s4, 4, 0)
            s16 = s8 + pltpu.roll(s8, 8, 0)
            mixed = []
            for gi, (win, s) in enumerate(zip(POOL_WINDOWS, (s2, s4, s8, s16))):
                lo = gi * POOL_GROUP_DIM
                cnt = jnp.minimum(tok1, win).astype(_F32) if is_meta else float(win)
                pooled = s[POOL_HALO:, lo:lo + POOL_GROUP_DIM] / cnt
                mixed.append((pooled - u[:, lo:lo + POOL_GROUP_DIM]).astype(_BF16))
            return mixed

        def pool_proj(mixed):
            mixed = [_dot(mx, wgrp_ref[gi]) for gi, mx in enumerate(mixed)]
            mixed = (jnp.concatenate(mixed, axis=-1) * pscale_ref[...]).astype(_BF16)
            return _dot(mixed, wpp_ref[...])

        def gate_chunk(c):
            return _sigmoid(_dot(hn, win_ref[:, C_GP + c * GATE_CHUNK:C_GP + (c + 1) * GATE_CHUNK]))

        n_gate_chunks = (C_END - C_GP) // GATE_CHUNK
        side = {}
        side_work = [lambda: side.__setitem__("mixed", pool_sums()),
                     lambda: side.__setitem__("y_pool", pool_proj(side["mixed"]))]
        side_work += [functools.partial(lambda c: side.__setitem__(("gate", c), gate_chunk(c)), c)
                      for c in range(n_gate_chunks)]

        q_idx = lax.broadcasted_iota(jnp.int32, (1, BLOCK), 1)
        if is_meta:
            key_idx = lax.broadcasted_iota(jnp.int32, (BLOCK, 1), 0)
            masks = [((key_idx <= q_idx) & (key_idx < N_META), None)]
        else:
            key_idx = lax.broadcasted_iota(jnp.int32, (N_KEYS, 1), 0)
            first_block_ok = key_idx >= jnp.where(i > 0, 0, BLOCK)
            masks = [(first_block_ok, None)] + [(None, None)] * (n_blocks - 1)
        zeros_q = jnp.zeros((HEAD_DIM, GQA_GROUP * BLOCK), _BF16)

        def scores(b, kh):
            r0 = b * BLOCK
            lanes = slice((kh // 2) * LANES, (kh // 2 + 1) * LANES)
            if is_meta:
                kk = k[:, lanes]
            else:
                k_prv = k_carry if b == 0 else k[r0 - BLOCK:r0, :]
                kk = jnp.concatenate([k_prv[:, lanes], k[r0:r0 + BLOCK, lanes], k_meta[:, lanes]], axis=0)
            q_stack = jnp.concatenate(
                [q_t[(kh * GQA_GROUP + g) * HEAD_DIM:(kh * GQA_GROUP + g + 1) * HEAD_DIM, r0:r0 + BLOCK]
                 for g in range(GQA_GROUP)], axis=1)
            q_z = jnp.concatenate([q_stack, zeros_q] if kh % 2 == 0 else [zeros_q, q_stack], axis=0)
            return _dot(kk, q_z)

        def attend(b, kh, s_t):
            r0 = b * BLOCK
            ok_b, _ = masks[b]
            rows_v = slice(kh * HEAD_DIM, (kh + 1) * HEAD_DIM)
            if is_meta:
                vv_t = v_t[rows_v, :]
            else:
                v_prv_t = v_carry_t if b == 0 else v_t[:, r0 - BLOCK:r0]
                vv_t = jnp.concatenate(
                    [v_prv_t[rows_v, :], v_t[rows_v, r0:r0 + BLOCK], v_meta_t[rows_v, :]], axis=1)
            probs, inv_l = [], []
            for g in range(GQA_GROUP):
                head = kh * GQA_GROUP + g
                logits = s_t[:, g * BLOCK:(g + 1) * BLOCK]
                if not is_meta:
                    logits = logits + bias_ref[head]
                if ok_b is not None:
                    logits = jnp.where(ok_b, logits, NEG_INF)
                sink = sinks_ref[layer, head] * LOG2_E
                m = jnp.maximum(jnp.max(logits, axis=0, keepdims=True), sink)
                p = jnp.exp2(logits - m)
                denom = jnp.sum(p, axis=0, keepdims=True) + jnp.exp2(sink - m)
                probs.append(p.astype(_BF16))
                inv_l.append(1.0 / denom)
            return vv_t, jnp.concatenate(probs, axis=1), inv_l

        def weighted_values(vv_t, p_t, inv_l):
            o_t = _dot(vv_t, p_t)
            return [o_t[:, g * BLOCK:(g + 1) * BLOCK] * inv_l[g] for g in range(GQA_GROUP)]

        head_rows = {b: [] for b in range(n_blocks)}

        def post_work(b_lo, b_hi):
            r0, r1 = b_lo * BLOCK, b_hi * BLOCK
            st = {}

            def y_piece(c):
                if "attn_t" not in st:
                    st["attn_t"] = jnp.concatenate(
                        [jnp.concatenate(head_rows[b], axis=0) for b in range(b_lo, b_hi)], axis=1).astype(_BF16)
                st["y", c] = _dot_tn(st["attn_t"], wap_ref[:, c * POST_CHUNK:(c + 1) * POST_CHUNK])

            def z_piece(c):
                if "merged" not in st:
                    y_attn = jnp.concatenate([st["y", cc] for cc in range(n_post_chunks)], axis=-1)
                    g_pool = jnp.concatenate([side["gate", cc][r0:r1] for cc in range(n_gate_chunks // 2)], axis=-1)
                    g_attn = jnp.concatenate(
                        [side["gate", cc][r0:r1] for cc in range(n_gate_chunks // 2, n_gate_chunks)], axis=-1)
                    st["merged"] = (g_pool * side["y_pool"][r0:r1] + g_attn * y_attn).astype(_BF16)
                st["z", c] = _dot(st["merged"], wo_ref[:, c * POST_CHUNK:(c + 1) * POST_CHUNK])

            def finish():
                z = jnp.concatenate([st["z", cc] for cc in range(n_post_chunks)], axis=-1)
                store(r0, r1, h[r0:r1] + _rmsnorm(z, g_ref[MIX_POST:MIX_POST + 1, :]))

            n_post_chunks = D_MODEL // POST_CHUNK
            return ([functools.partial(y_piece, c) for c in range(n_post_chunks)]
                    + [functools.partial(z_piece, c) for c in range(n_post_chunks)] + [finish])

        n_groups = 2 if n_blocks % 2 == 0 else 1
        blocks_per_group = n_blocks // n_groups
        steps = [(b, kh) for b in range(n_blocks) for kh in range(N_KV_HEADS)]
        steps_per_group = blocks_per_group * N_KV_HEADS
        queue = list(side_work)
        queue.pop(0)()
        s_next = scores(*steps[0])
        v_t = _proj_t(win_ref[:, C_V:C_GP], hn).astype(_BF16)
        if is_meta:
            vmeta_ref[...] = v_t[:, :N_META]
        else:
            v_meta_t = vmeta_ref[...]
            v_carry_t = vprev_ref[rd]
            vprev_ref[wr] = v_t[:, rows - BLOCK:]
        pending = None
        for n, (b, kh) in enumerate(steps):
            s_cur = s_next
            if n + 1 < len(steps):
                s_next = scores(*steps[n + 1])
            soft = attend(b, kh, s_cur)
            if pending is not None:
                head_rows[pending[0]] += weighted_values(*pending[1])
            steps_left = steps_per_group - n % steps_per_group
            pending = (b, soft) if steps_left > 1 else None
            if pending is None:
                head_rows[b] += weighted_values(*soft)
            for _ in range(-(-len(queue) // steps_left)):
                queue.pop(0)()
            if steps_left == 1:
                grp = n // steps_per_group
                queue += post_work(grp * blocks_per_group, (grp + 1) * blocks_per_group)
        while queue:
            queue.pop(0)()

    def store_meta(r0, r1, value):
        om_ref[...] = value[:N_META, :]

    def store_rows(r0, r1, value):
        o_ref[r0:r1, :] = value

    @pl.when(i == 0)
    def _():
        kprev_ref[0] = jnp.zeros((BLOCK, KV_DIM), _BF16)
        vprev_ref[0] = jnp.zeros((KV_DIM, BLOCK), _BF16)
        key_idx = lax.broadcasted_iota(jnp.int32, (N_KEYS, 1), 0)
        q_idx = lax.broadcasted_iota(jnp.int32, (1, BLOCK), 1)
        in_blk = jnp.bitwise_and(key_idx, BLOCK - 1)
        is_meta_key = key_idx >= 2 * BLOCK
        dist = jnp.where(key_idx < BLOCK, q_idx + BLOCK - in_blk, q_idx - in_blk)
        ok = ((dist >= 0) & (dist < WINDOW)) | is_meta_key
        alibi_d = jnp.where(is_meta_key, 0, dist).astype(_F32)
        for head in range(N_HEADS):
            bias_ref[head] = jnp.where(ok, (-_alibi_slope(head) * LOG2_E) * alibi_d, NEG_INF)
        hm = jnp.concatenate([hm_ref[...], jnp.zeros((BLOCK - N_META, D_MODEL), _F32)], axis=0)
        mixer_rows(hm, True, store_meta)

    mixer_rows(h_ref[...], False, store_rows)


def _mixer_call(h, hm, norm_g, w_in, w_grp, pscale, w_pp, sinks, w_ap, w_o, layer, cast_next):
    n_steps = h.shape[0] // MIXER_TILE_ROWS
    row_spec = pl.BlockSpec((MIXER_TILE_ROWS, D_MODEL), lambda i: (i, 0))
    plans = [_cast_plan(w, index, n_steps) for w, index in cast_next]
    return pl.pallas_call(
        functools.partial(_mixer_kernel, layer, len(plans)),
        grid=(n_steps,),
        in_specs=[
            row_spec,
            _const_spec((N_META, D_MODEL), (0, 0)),
            _const_spec((None,) + norm_g.shape[1:], (layer, 0, 0)),
            _const_spec(w_in.shape, (0, 0)),
            _const_spec(w_grp.shape, (0, 0, 0)),
            _const_spec((None,) + pscale.shape[1:], (layer, 0, 0)),
            _const_spec(w_pp.shape, (0, 0)),
            pl.BlockSpec(memory_space=pltpu.SMEM),
            _const_spec(w_ap.shape, (0, 0)),
            _const_spec(w_o.shape, (0, 0)),
        ] + [p[0] for p in plans],
        out_specs=[row_spec, pl.BlockSpec((N_META, D_MODEL), lambda i: (0, 0))] + [p[1] for p in plans],
        out_shape=[jax.ShapeDtypeStruct(h.shape, h.dtype), jax.ShapeDtypeStruct(hm.shape, hm.dtype)]
        + [p[2] for p in plans],
        scratch_shapes=[
            pltpu.VMEM((2, POOL_HALO, POOL_DIM), _F32),
            pltpu.VMEM((2, BLOCK, KV_DIM), _BF16),
            pltpu.VMEM((2, KV_DIM, BLOCK), _BF16),
            pltpu.VMEM((N_META, KV_DIM), _BF16),
            pltpu.VMEM((KV_DIM, N_META), _BF16),
            pltpu.VMEM((N_HEADS, N_KEYS, BLOCK), _F32),
        ],
        compiler_params=pltpu.CompilerParams(
            dimension_semantics=("arbitrary",), vmem_limit_bytes=VMEM_LIMIT),
        name="gated_mixers",
    )(h, hm, norm_g, w_in, w_grp, pscale, w_pp, sinks, w_ap, w_o, *[w for w, _ in cast_next])


def kernel(x, meta_tokens, norm_g, ffn_w_in, ffn_w_out, w_in, pool_w_grp, pool_scale, w_pool_proj, sinks,
           w_attn_proj, w_o):
    batch, seq, d = x.shape
    assert batch == 1 and d == D_MODEL and seq % FFN_TILE_ROWS == 0 and seq % MIXER_TILE_ROWS == 0
    assert meta_tokens.shape == (N_META, D_MODEL) and POOL_HALO == N_META
    depth = norm_g.shape[0]
    n_groups = len(POOL_WINDOWS)
    w_grp_v = pool_w_grp.reshape(depth, n_groups * POOL_GROUP_DIM, POOL_GROUP_DIM)
    pscale = pool_scale.reshape(depth, 1, POOL_DIM)

    def ffn_weights(l, j):
        return [(ffn_w_in, (l, j)), (ffn_w_out, (l, j))]

    def mixer_weights(l):
        return [(w_in, (l,)), (w_grp_v, (l,)), (w_pool_proj, (l,)), (w_attn_proj, (l,)), (w_o, (l,))]

    h, hm = x[0], meta_tokens.astype(x.dtype)
    w_ffn1 = _cast_call(ffn_weights(0, 0), FIRST_CAST_STEPS)
    w_mix = None
    for l in range(depth):
        last = l + 1 == depth
        h, hm, *cast = _ffn_call(h, hm, norm_g, w_ffn1[0], w_ffn1[1], l, FFN1_PRE, FFN1_POST,
                                 mixer_weights(l) if w_mix is None else [])
        w_mix = cast or w_mix
        w_grp = w_mix[1].reshape(n_groups, POOL_GROUP_DIM, POOL_GROUP_DIM)
        ahead = ffn_weights(l, 1) + ([] if last else ffn_weights(l + 1, 0) + mixer_weights(l + 1))
        h, hm, *cast = _mixer_call(h, hm, norm_g, w_mix[0], w_grp, pscale, w_mix[2], sinks, w_mix[3], w_mix[4],
                                   l, ahead)
        h, hm = _ffn_call(h, hm, norm_g, cast[0], cast[1], l, FFN2_PRE, FFN2_POST, [])
        w_ffn1, w_mix = cast[2:4], cast[4:]
    return h[None]
```

```python
import functools

import jax
import jax.numpy as jnp
from jax import lax
from jax.experimental import pallas as pl
from jax.experimental.pallas import tpu as pltpu

D_MODEL = 1024
N_META = 16
N_HEADS = 16
N_KV_HEADS = 4
HEAD_DIM = 64
GQA_GROUP = N_HEADS // N_KV_HEADS
WINDOW = 128
LANES = 128
BLOCK = 128
POOL_WINDOWS = (2, 4, 8, 16)
POOL_GROUP_DIM = 128
POOL_DIM = 512
KV_DIM = 256
N_KEYS = 2 * BLOCK + N_META
D_FF = 2816
RMS_EPS = 1e-6
NEG_INF = -1e30
LOG2_E = 1.4426950408889634
FFN1_PRE, FFN1_POST, MIX_PRE, MIX_POST, FFN2_PRE, FFN2_POST = range(6)

C_U, C_Q, C_K, C_V, C_GP, C_GA, C_END = 0, 512, 1536, 1792, 2048, 3072, 4096

GATE_CHUNK = 256
POOL_HALO = 16
MIXER_TILE_ROWS = 512
POST_CHUNK = 256
FFN_TILE_ROWS = 1024
FFN_SUB_ROWS = 256
VMEM_LIMIT = 56 * 1024 * 1024
FIRST_CAST_STEPS = 16
BF16_SUBLANES = 16

_BF16 = jnp.bfloat16
_F32 = jnp.float32


def _rmsnorm(x, g):
    ms = jnp.mean(x * x, axis=-1, keepdims=True)
    return x * lax.rsqrt(ms + RMS_EPS) * g


def _sigmoid(x):
    return 0.5 * jnp.tanh(0.5 * x) + 0.5


def _dot(a, b):
    return jnp.dot(a, b, preferred_element_type=_F32)


def _dot_tn(a, b):
    return lax.dot_general(a, b, (((0,), (0,)), ((), ())), preferred_element_type=_F32)


def _proj_t(w, x):
    return lax.dot_general(w, x, (((0,), (1,)), ((), ())), preferred_element_type=_F32)


def _const_spec(block_shape, index):
    return pl.BlockSpec(block_shape, lambda i: index, pipeline_mode=pl.Buffered(1))


def _cast_plan(w, index, n_steps):
    rows, cols = w.shape[len(index):]
    n_blocks = max(d for d in range(1, n_steps + 1) if n_steps % d == 0 and rows % (d * BF16_SUBLANES) == 0)
    block_rows, revisit = rows // n_blocks, n_steps // n_blocks
    in_spec = pl.BlockSpec((None,) * len(index) + (block_rows, cols), lambda i: tuple(index) + (i // revisit, 0))
    out_spec = pl.BlockSpec((block_rows, cols), lambda i: (i // revisit, 0))
    return in_spec, out_spec, jax.ShapeDtypeStruct((rows, cols), _BF16)


def _cast_blocks(src_refs, dst_refs):
    for src, dst in zip(src_refs, dst_refs):
        dst[...] = src[...].astype(dst.dtype)


def _cast_kernel(*refs):
    _cast_blocks(refs[:len(refs) // 2], refs[len(refs) // 2:])


def _cast_call(weights, n_steps):
    plans = [_cast_plan(w, index, n_steps) for w, index in weights]
    return pl.pallas_call(
        _cast_kernel,
        grid=(n_steps,),
        in_specs=[p[0] for p in plans],
        out_specs=[p[1] for p in plans],
        out_shape=[p[2] for p in plans],
        compiler_params=pltpu.CompilerParams(dimension_semantics=("arbitrary",), vmem_limit_bytes=VMEM_LIMIT),
        name="cast_bf16",
    )(*[w for w, _ in weights])


def _ffn_rows(h, g_ref, win_ref, wout_ref, pre, post):
    xn = _rmsnorm(h, g_ref[pre:pre + 1, :]).astype(_BF16)
    gate = _dot(xn, win_ref[:, :D_FF])
    up = _dot(xn, win_ref[:, D_FF:])
    half = 0.5 * gate
    act = (half * (jnp.tanh(half) + 1.0) * up).astype(_BF16)
    y = _dot(act, wout_ref[...])
    return h + 0.5 * _rmsnorm(y, g_ref[post:post + 1, :])


def _ffn_kernel(pre, post, n_cast, *refs):
    h_ref, hm_ref, g_ref, win_ref, wout_ref = refs[:5]
    o_ref, om_ref = refs[5 + n_cast:7 + n_cast]
    _cast_blocks(refs[5:5 + n_cast], refs[7 + n_cast:])

    @pl.when(pl.program_id(0) == 0)
    def _():
        om_ref[...] = _ffn_rows(hm_ref[...], g_ref, win_ref, wout_ref, pre, post)

    for r0 in range(0, FFN_TILE_ROWS, FFN_SUB_ROWS):
        o_ref[r0:r0 + FFN_SUB_ROWS, :] = _ffn_rows(
            h_ref[r0:r0 + FFN_SUB_ROWS, :], g_ref, win_ref, wout_ref, pre, post)


def _ffn_call(h, hm, norm_g, w_in, w_out, layer, pre, post, cast_next):
    n_steps = h.shape[0] // FFN_TILE_ROWS
    row_spec = pl.BlockSpec((FFN_TILE_ROWS, D_MODEL), lambda i: (i, 0))
    plans = [_cast_plan(w, index, n_steps) for w, index in cast_next]
    return pl.pallas_call(
        functools.partial(_ffn_kernel, pre, post, len(plans)),
        grid=(n_steps,),
        in_specs=[
            row_spec,
            _const_spec((N_META, D_MODEL), (0, 0)),
            _const_spec((None,) + norm_g.shape[1:], (layer, 0, 0)),
            _const_spec(w_in.shape, (0, 0)),
            _const_spec(w_out.shape, (0, 0)),
        ] + [p[0] for p in plans],
        out_specs=[row_spec, pl.BlockSpec((N_META, D_MODEL), lambda i: (0, 0))] + [p[1] for p in plans],
        out_shape=[jax.ShapeDtypeStruct(h.shape, h.dtype), jax.ShapeDtypeStruct(hm.shape, hm.dtype)]
        + [p[2] for p in plans],
        compiler_params=pltpu.CompilerParams(
            dimension_semantics=("arbitrary",), vmem_limit_bytes=VMEM_LIMIT),
        name="ffn_half_step",
    )(h, hm, norm_g, w_in, w_out, *[w for w, _ in cast_next])


def _alibi_slope(head):
    return 2.0 ** (-8.0 * (head + 1) / N_HEADS)


def _mixer_kernel(layer, n_cast, *refs):
    h_ref, hm_ref, g_ref, win_ref, wgrp_ref, pscale_ref, wpp_ref, sinks_ref, wap_ref, wo_ref = refs[:10]
    o_ref, om_ref = refs[10 + n_cast:12 + n_cast]
    ucarry_ref, kprev_ref, vprev_ref, kmeta_ref, vmeta_ref, bias_ref = refs[12 + 2 * n_cast:]
    _cast_blocks(refs[10:10 + n_cast], refs[12 + n_cast:12 + 2 * n_cast])
    i = pl.program_id(0)
    rd = lax.rem(i, 2)
    wr = 1 - rd

    def mixer_rows(h, is_meta, store):
        rows = h.shape[0]
        n_blocks = rows // BLOCK
        hn = _rmsnorm(h, g_ref[MIX_PRE:MIX_PRE + 1, :]).astype(_BF16)
        row_idx = lax.broadcasted_iota(jnp.int32, (rows, 1), 0)

        k = _dot(hn, win_ref[:, C_K:C_V]).astype(_BF16)
        q_t = (_proj_t(win_ref[:, C_Q:C_K], hn) * (HEAD_DIM ** -0.5 * LOG2_E)).astype(_BF16)

        if is_meta:
            kmeta_ref[...] = k[:N_META, :]
        else:
            k_meta = kmeta_ref[...]
            k_carry = kprev_ref[rd]
            kprev_ref[wr] = k[rows - BLOCK:, :]

        def pool_sums():
            u = _dot(hn, win_ref[:, C_U:C_Q])
            if is_meta:
                u = jnp.where(row_idx < N_META, u, 0.0)
                history = jnp.zeros((POOL_HALO, POOL_DIM), _F32)
                ucarry_ref[0] = u[:POOL_HALO, :]
                tok1 = jnp.minimum(row_idx + 1, POOL_WINDOWS[-1])
            else:
                history = ucarry_ref[rd]
                ucarry_ref[wr] = u[rows - POOL_HALO:, :]
            e = jnp.concatenate([history, u], axis=0)
            mixed, level = [], e
            for gi, win in enumerate(POOL_WINDOWS):
                level = level + pltpu.roll(level, win // 2, 0)
                lo = gi * POOL_GROUP_DIM
                window_sum = level[POOL_HALO:, :POOL_GROUP_DIM]
                level = level[:, POOL_GROUP_DIM:]
                pooled = window_sum / jnp.minimum(tok1, win).astype(_F32) if is_meta else window_sum * (1.0 / win)
                mixed.append((pooled - u[:, lo:lo + POOL_GROUP_DIM]).astype(_BF16))
            return mixed

        def pool_proj(mixed):
            mixed = [_dot(mx, wgrp_ref[gi]) for gi, mx in enumerate(mixed)]
            mixed = (jnp.concatenate(mixed, axis=-1) * pscale_ref[...]).astype(_BF16)
            return _dot(mixed, wpp_ref[...])

        def gate_chunk(c):
            return _sigmoid(_dot(hn, win_ref[:, C_GP + c * GATE_CHUNK:C_GP + (c + 1) * GATE_CHUNK]))

        n_gate_chunks = (C_END - C_GP) // GATE_CHUNK
        side = {}
        side_work = [lambda: side.__setitem__("mixed", pool_sums()),
                     lambda: side.__setitem__("y_pool", pool_proj(side["mixed"]))]
        side_work += [functools.partial(lambda c: side.__setitem__(("gate", c), gate_chunk(c)), c)
                      for c in range(n_gate_chunks)]

        q_idx = lax.broadcasted_iota(jnp.int32, (1, BLOCK), 1)
        if is_meta:
            key_idx = lax.broadcasted_iota(jnp.int32, (BLOCK, 1), 0)
            masks = [((key_idx <= q_idx) & (key_idx < N_META), None)]
        else:
            masks = [(None, jnp.where(i > 0, 0, N_HEADS))] + [(None, 0)] * (n_blocks - 1)
        zeros_q = jnp.zeros((HEAD_DIM, GQA_GROUP * BLOCK), _BF16)

        def scores(b, kh):
            r0 = b * BLOCK
            lanes = slice((kh // 2) * LANES, (kh // 2 + 1) * LANES)
            if is_meta:
                kk = k[:, lanes]
            else:
                k_prv = k_carry if b == 0 else k[r0 - BLOCK:r0, :]
                kk = jnp.concatenate([k_prv[:, lanes], k[r0:r0 + BLOCK, lanes], k_meta[:, lanes]], axis=0)
            q_stack = jnp.concatenate(
                [q_t[(kh * GQA_GROUP + g) * HEAD_DIM:(kh * GQA_GROUP + g + 1) * HEAD_DIM, r0:r0 + BLOCK]
                 for g in range(GQA_GROUP)], axis=1)
            q_z = jnp.concatenate([q_stack, zeros_q] if kh % 2 == 0 else [zeros_q, q_stack], axis=0)
            return _dot(kk, q_z)

        def attend(b, kh, s_t):
            r0 = b * BLOCK
            ok_b, bias_slot = masks[b]
            rows_v = slice(kh * HEAD_DIM, (kh + 1) * HEAD_DIM)
            if is_meta:
                vv_t = v_t[rows_v, :]
            else:
                v_prv_t = v_carry_t if b == 0 else v_t[:, r0 - BLOCK:r0]
                vv_t = jnp.concatenate(
                    [v_prv_t[rows_v, :], v_t[rows_v, r0:r0 + BLOCK], v_meta_t[rows_v, :]], axis=1)
            probs, inv_l = [], []
            for g in range(GQA_GROUP):
                head = kh * GQA_GROUP + g
                logits = s_t[:, g * BLOCK:(g + 1) * BLOCK]
                if not is_meta:
                    logits = logits + bias_ref[bias_slot + head]
                if ok_b is not None:
                    logits = jnp.where(ok_b, logits, NEG_INF)
                sink = sinks_ref[layer, head] * LOG2_E
                m = jnp.maximum(jnp.max(logits, axis=0, keepdims=True), sink)
                p = jnp.exp2(logits - m)
                denom = jnp.sum(p, axis=0, keepdims=True) + jnp.exp2(sink - m)
                probs.append(p.astype(_BF16))
                inv_l.append(1.0 / denom)
            return vv_t, jnp.concatenate(probs, axis=1), inv_l

        def weighted_values(vv_t, p_t, inv_l):
            o_t = _dot(vv_t, p_t)
            return [o_t[:, g * BLOCK:(g + 1) * BLOCK] * inv_l[g] for g in range(GQA_GROUP)]

        head_rows = {b: [] for b in range(n_blocks)}

        def post_work(b_lo, b_hi):
            r0, r1 = b_lo * BLOCK, b_hi * BLOCK
            st = {}

            def y_piece(c):
                if "attn_t" not in st:
                    st["attn_t"] = jnp.concatenate(
                        [jnp.concatenate(head_rows[b], axis=0) for b in range(b_lo, b_hi)], axis=1).astype(_BF16)
                st["y", c] = _dot_tn(st["attn_t"], wap_ref[:, c * POST_CHUNK:(c + 1) * POST_CHUNK])

            def z_piece(c):
                if "merged" not in st:
                    y_attn = jnp.concatenate([st["y", cc] for cc in range(n_post_chunks)], axis=-1)
                    g_pool = jnp.concatenate([side["gate", cc][r0:r1] for cc in range(n_gate_chunks // 2)], axis=-1)
                    g_attn = jnp.concatenate(
                        [side["gate", cc][r0:r1] for cc in range(n_gate_chunks // 2, n_gate_chunks)], axis=-1)
                    st["merged"] = (g_pool * side["y_pool"][r0:r1] + g_attn * y_attn).astype(_BF16)
                st["z", c] = _dot(st["merged"], wo_ref[:, c * POST_CHUNK:(c + 1) * POST_CHUNK])

            def finish():
                z = jnp.concatenate([st["z", cc] for cc in range(n_post_chunks)], axis=-1)
                store(r0, r1, h[r0:r1] + _rmsnorm(z, g_ref[MIX_POST:MIX_POST + 1, :]))

            n_post_chunks = D_MODEL // POST_CHUNK
            return ([functools.partial(y_piece, c) for c in range(n_post_chunks)]
                    + [functools.partial(z_piece, c) for c in range(n_post_chunks)] + [finish])

        n_groups = 2 if n_blocks % 2 == 0 else 1
        blocks_per_group = n_blocks // n_groups
        steps = [(b, kh) for b in range(n_blocks) for kh in range(N_KV_HEADS)]
        steps_per_group = blocks_per_group * N_KV_HEADS
        queue = list(side_work)
        queue.pop(0)()
        s_next = scores(*steps[0])
        v_t = _proj_t(win_ref[:, C_V:C_GP], hn).astype(_BF16)
        if is_meta:
            vmeta_ref[...] = v_t[:, :N_META]
        else:
            v_meta_t = vmeta_ref[...]
            v_carry_t = vprev_ref[rd]
            vprev_ref[wr] = v_t[:, rows - BLOCK:]
        pending = None
        for n, (b, kh) in enumerate(steps):
            s_cur = s_next
            if n + 1 < len(steps):
                s_next = scores(*steps[n + 1])
            soft = attend(b, kh, s_cur)
            if pending is not None:
                head_rows[pending[0]] += weighted_values(*pending[1])
            steps_left = steps_per_group - n % steps_per_group
            pending = (b, soft) if steps_left > 1 else None
            if pending is None:
                head_rows[b] += weighted_values(*soft)
            for _ in range(-(-len(queue) // steps_left)):
                queue.pop(0)()
            if steps_left == 1:
                grp = n // steps_per_group
                queue += post_work(grp * blocks_per_group, (grp + 1) * blocks_per_group)
        while queue:
            queue.pop(0)()

    def store_meta(r0, r1, value):
        om_ref[...] = value[:N_META, :]

    def store_rows(r0, r1, value):
        o_ref[r0:r1, :] = value

    @pl.when(i == 0)
    def _():
        kprev_ref[0] = jnp.zeros((BLOCK, KV_DIM), _BF16)
        vprev_ref[0] = jnp.zeros((KV_DIM, BLOCK), _BF16)
        key_idx = lax.broadcasted_iota(jnp.int32, (N_KEYS, 1), 0)
        q_idx = lax.broadcasted_iota(jnp.int32, (1, BLOCK), 1)
        in_blk = jnp.bitwise_and(key_idx, BLOCK - 1)
        is_meta_key = key_idx >= 2 * BLOCK
        dist = jnp.where(key_idx < BLOCK, q_idx + BLOCK - in_blk, q_idx - in_blk)
        ok = ((dist >= 0) & (dist < WINDOW)) | is_meta_key
        alibi_d = jnp.where(is_meta_key, 0, dist).astype(_F32)
        no_prev = ok & (key_idx >= BLOCK)
        for head in range(N_HEADS):
            bias = (-_alibi_slope(head) * LOG2_E) * alibi_d
            bias_ref[head] = jnp.where(ok, bias, NEG_INF)
            bias_ref[N_HEADS + head] = jnp.where(no_prev, bias, NEG_INF)
        hm = jnp.concatenate([hm_ref[...], jnp.zeros((BLOCK - N_META, D_MODEL), _F32)], axis=0)
        mixer_rows(hm, True, store_meta)

    mixer_rows(h_ref[...], False, store_rows)


def _mixer_call(h, hm, norm_g, w_in, w_grp, pscale, w_pp, sinks, w_ap, w_o, layer, cast_next):
    n_steps = h.shape[0] // MIXER_TILE_ROWS
    row_spec = pl.BlockSpec((MIXER_TILE_ROWS, D_MODEL), lambda i: (i, 0))
    plans = [_cast_plan(w, index, n_steps) for w, index in cast_next]
    return pl.pallas_call(
        functools.partial(_mixer_kernel, layer, len(plans)),
        grid=(n_steps,),
        in_specs=[
            row_spec,
            _const_spec((N_META, D_MODEL), (0, 0)),
            _const_spec((None,) + norm_g.shape[1:], (layer, 0, 0)),
            _const_spec(w_in.shape, (0, 0)),
            _const_spec(w_grp.shape, (0, 0, 0)),
            _const_spec((None,) + pscale.shape[1:], (layer, 0, 0)),
            _const_spec(w_pp.shape, (0, 0)),
            pl.BlockSpec(memory_space=pltpu.SMEM),
            _const_spec(w_ap.shape, (0, 0)),
            _const_spec(w_o.shape, (0, 0)),
        ] + [p[0] for p in plans],
        out_specs=[row_spec, pl.BlockSpec((N_META, D_MODEL), lambda i: (0, 0))] + [p[1] for p in plans],
        out_shape=[jax.ShapeDtypeStruct(h.shape, h.dtype), jax.ShapeDtypeStruct(hm.shape, hm.dtype)]
        + [p[2] for p in plans],
        scratch_shapes=[
            pltpu.VMEM((2, POOL_HALO, POOL_DIM), _F32),
            pltpu.VMEM((2, BLOCK, KV_DIM), _BF16),
            pltpu.VMEM((2, KV_DIM, BLOCK), _BF16),
            pltpu.VMEM((N_META, KV_DIM), _BF16),
            pltpu.VMEM((KV_DIM, N_META), _BF16),
            pltpu.VMEM((2 * N_HEADS, N_KEYS, BLOCK), _F32),
        ],
        compiler_params=pltpu.CompilerParams(
            dimension_semantics=("arbitrary",), vmem_limit_bytes=VMEM_LIMIT),
        name="gated_mixers",
    )(h, hm, norm_g, w_in, w_grp, pscale, w_pp, sinks, w_ap, w_o, *[w for w, _ in cast_next])


def kernel(x, meta_tokens, norm_g, ffn_w_in, ffn_w_out, w_in, pool_w_grp, pool_scale, w_pool_proj, sinks,
           w_attn_proj, w_o):
    batch, seq, d = x.shape
    assert batch == 1 and d == D_MODEL and seq % FFN_TILE_ROWS == 0 and seq % MIXER_TILE_ROWS == 0
    assert meta_tokens.shape == (N_META, D_MODEL) and POOL_HALO == N_META
    depth = norm_g.shape[0]
    n_groups = len(POOL_WINDOWS)
    w_grp_v = pool_w_grp.reshape(depth, n_groups * POOL_GROUP_DIM, POOL_GROUP_DIM)
    pscale = pool_scale.reshape(depth, 1, POOL_DIM)

    def ffn_weights(l, j):
        return [(ffn_w_in, (l, j)), (ffn_w_out, (l, j))]

    def mixer_weights(l):
        return [(w_in, (l,)), (w_grp_v, (l,)), (w_pool_proj, (l,)), (w_attn_proj, (l,)), (w_o, (l,))]

    h, hm = x[0], meta_tokens.astype(x.dtype)
    w_ffn1 = _cast_call(ffn_weights(0, 0), FIRST_CAST_STEPS)
    w_mix = None
    for l in range(depth):
        last = l + 1 == depth
        h, hm, *cast = _ffn_call(h, hm, norm_g, w_ffn1[0], w_ffn1[1], l, FFN1_PRE, FFN1_POST,
                                 mixer_weights(l) if w_mix is None else [])
        w_mix = cast or w_mix
        w_grp = w_mix[1].reshape(n_groups, POOL_GROUP_DIM, POOL_GROUP_DIM)
        ahead = ffn_weights(l, 1) + ([] if last else ffn_weights(l + 1, 0) + mixer_weights(l + 1))
        h, hm, *cast = _mixer_call(h, hm, norm_g, w_mix[0], w_grp, pscale, w_mix[2], sinks, w_mix[3], w_mix[4],
                                   l, ahead)
        h, hm = _ffn_call(h, hm, norm_g, cast[0], cast[1], l, FFN2_PRE, FFN2_POST, [])
        w_ffn1, w_mix = cast[2:4], cast[4:]
    return h[None]
```

```python
import functools

import jax
import jax.numpy as jnp
from jax import lax
from jax.experimental import pallas as pl
from jax.experimental.pallas import tpu as pltpu

D_MODEL = 1024
N_META = 16
N_HEADS = 16
N_KV_HEADS = 4
HEAD_DIM = 64
GQA_GROUP = N_HEADS // N_KV_HEADS
WINDOW = 128
LANES = 128
BLOCK = 128
POOL_WINDOWS = (2, 4, 8, 16)
POOL_GROUP_DIM = 128
POOL_DIM = 512
KV_DIM = 256
N_KEYS = 2 * BLOCK + N_META
D_FF = 2816
RMS_EPS = 1e-6
NEG_INF = -1e30
LOG2_E = 1.4426950408889634
FFN1_PRE, FFN1_POST, MIX_PRE, MIX_POST, FFN2_PRE, FFN2_POST = range(6)

C_U, C_Q, C_K, C_V, C_GP, C_GA, C_END = 0, 512, 1536, 1792, 2048, 3072, 4096

GATE_CHUNK = 256
POOL_HALO = 16
MIXER_TILE_ROWS = 512
POST_CHUNK = 256
FFN_TILE_ROWS = 1024
FFN_SUB_ROWS = 256
VMEM_LIMIT = 56 * 1024 * 1024
FIRST_CAST_STEPS = 16
BF16_SUBLANES = 16

_BF16 = jnp.bfloat16
_F32 = jnp.float32


def _rmsnorm(x, g):
    ms = jnp.mean(x * x, axis=-1, keepdims=True)
    return x * lax.rsqrt(ms + RMS_EPS) * g


def _sigmoid(x):
    return 0.5 * jnp.tanh(0.5 * x) + 0.5


def _dot(a, b):
    return jnp.dot(a, b, preferred_element_type=_F32)


def _dot_tn(a, b):
    return lax.dot_general(a, b, (((0,), (0,)), ((), ())), preferred_element_type=_F32)


def _proj_t(w, x):
    return lax.dot_general(w, x, (((0,), (1,)), ((), ())), preferred_element_type=_F32)


def _const_spec(block_shape, index):
    return pl.BlockSpec(block_shape, lambda i: index, pipeline_mode=pl.Buffered(1))


def _cast_plan(w, index, n_steps):
    rows, cols = w.shape[len(index):]
    n_blocks = max(d for d in range(1, n_steps + 1) if n_steps % d == 0 and rows % (d * BF16_SUBLANES) == 0)
    block_rows, revisit = rows // n_blocks, n_steps // n_blocks
    in_spec = pl.BlockSpec((None,) * len(index) + (block_rows, cols), lambda i: tuple(index) + (i // revisit, 0))
    out_spec = pl.BlockSpec((block_rows, cols), lambda i: (i // revisit, 0))
    return in_spec, out_spec, jax.ShapeDtypeStruct((rows, cols), _BF16)


def _cast_blocks(src_refs, dst_refs):
    for src, dst in zip(src_refs, dst_refs):
        dst[...] = src[...].astype(dst.dtype)


def _cast_kernel(*refs):
    _cast_blocks(refs[:len(refs) // 2], refs[len(refs) // 2:])


def _cast_call(weights, n_steps):
    plans = [_cast_plan(w, index, n_steps) for w, index in weights]
    return pl.pallas_call(
        _cast_kernel,
        grid=(n_steps,),
        in_specs=[p[0] for p in plans],
        out_specs=[p[1] for p in plans],
        out_shape=[p[2] for p in plans],
        compiler_params=pltpu.CompilerParams(dimension_semantics=("arbitrary",), vmem_limit_bytes=VMEM_LIMIT),
        name="cast_bf16",
    )(*[w for w, _ in weights])


def _ffn_rows(h, g_ref, win_ref, wout_ref, pre, post):
    xn = _rmsnorm(h, g_ref[pre:pre + 1, :]).astype(_BF16)
    gate = _dot(xn, win_ref[:, :D_FF])
    up = _dot(xn, win_ref[:, D_FF:])
    act = (gate * _sigmoid(gate) * up).astype(_BF16)
    y = _dot(act, wout_ref[...])
    return h + 0.5 * _rmsnorm(y, g_ref[post:post + 1, :])


def _ffn_kernel(pre, post, n_cast, *refs):
    h_ref, hm_ref, g_ref, win_ref, wout_ref = refs[:5]
    o_ref, om_ref = refs[5 + n_cast:7 + n_cast]
    _cast_blocks(refs[5:5 + n_cast], refs[7 + n_cast:])

    @pl.when(pl.program_id(0) == 0)
    def _():
        om_ref[...] = _ffn_rows(hm_ref[...], g_ref, win_ref, wout_ref, pre, post)

    for r0 in range(0, FFN_TILE_ROWS, FFN_SUB_ROWS):
        o_ref[r0:r0 + FFN_SUB_ROWS, :] = _ffn_rows(
            h_ref[r0:r0 + FFN_SUB_ROWS, :], g_ref, win_ref, wout_ref, pre, post)


def _ffn_call(h, hm, norm_g, w_in, w_out, layer, pre, post, cast_next):
    n_steps = h.shape[0] // FFN_TILE_ROWS
    row_spec = pl.BlockSpec((FFN_TILE_ROWS, D_MODEL), lambda i: (i, 0))
    plans = [_cast_plan(w, index, n_steps) for w, index in cast_next]
    return pl.pallas_call(
        functools.partial(_ffn_kernel, pre, post, len(plans)),
        grid=(n_steps,),
        in_specs=[
            row_spec,
            _const_spec((N_META, D_MODEL), (0, 0)),
            _const_spec((None,) + norm_g.shape[1:], (layer, 0, 0)),
            _const_spec(w_in.shape, (0, 0)),
            _const_spec(w_out.shape, (0, 0)),
        ] + [p[0] for p in plans],
        out_specs=[row_spec, pl.BlockSpec((N_META, D_MODEL), lambda i: (0, 0))] + [p[1] for p in plans],
        out_shape=[jax.ShapeDtypeStruct(h.shape, h.dtype), jax.ShapeDtypeStruct(hm.shape, hm.dtype)]
        + [p[2] for p in plans],
        compiler_params=pltpu.CompilerParams(
            dimension_semantics=("arbitrary",), vmem_limit_bytes=VMEM_LIMIT),
        name="ffn_half_step",
    )(h, hm, norm_g, w_in, w_out, *[w for w, _ in cast_next])


def _alibi_slope(head):
    return 2.0 ** (-8.0 * (head + 1) / N_HEADS)


def _mixer_kernel(layer, n_cast, *refs):
    h_ref, hm_ref, g_ref, win_ref, wgrp_ref, pscale_ref, wpp_ref, sinks_ref, wap_ref, wo_ref = refs[:10]
    o_ref, om_ref = refs[10 + n_cast:12 + n_cast]
    ucarry_ref, kprev_ref, vprev_ref, kmeta_ref, vmeta_ref, bias_ref = refs[12 + 2 * n_cast:]
    _cast_blocks(refs[10:10 + n_cast], refs[12 + n_cast:12 + 2 * n_cast])
    i = pl.program_id(0)
    rd = lax.rem(i, 2)
    wr = 1 - rd

    def mixer_rows(h, is_meta, store):
        rows = h.shape[0]
        n_blocks = rows // BLOCK
        hn = _rmsnorm(h, g_ref[MIX_PRE:MIX_PRE + 1, :]).astype(_BF16)
        row_idx = lax.broadcasted_iota(jnp.int32, (rows, 1), 0)

        k = _dot(hn, win_ref[:, C_K:C_V]).astype(_BF16)
        q_t = (_proj_t(win_ref[:, C_Q:C_K], hn) * (HEAD_DIM ** -0.5 * LOG2_E)).astype(_BF16)

        if is_meta:
            kmeta_ref[...] = k[:N_META, :]
        else:
            k_meta = kmeta_ref[...]
            k_carry = kprev_ref[rd]
            kprev_ref[wr] = k[rows - BLOCK:, :]

        def pool_sums():
            u = _dot(hn, win_ref[:, C_U:C_Q])
            if is_meta:
                u = jnp.where(row_idx < N_META, u, 0.0)
                history = jnp.zeros((POOL_HALO, POOL_DIM), _F32)
                ucarry_ref[0] = u[:POOL_HALO, :]
                tok1 = jnp.minimum(row_idx + 1, POOL_WINDOWS[-1])
            else:
                history = ucarry_ref[rd]
                ucarry_ref[wr] = u[rows - POOL_HALO:, :]
            e = jnp.concatenate([history, u], axis=0)
            s2 = e + pltpu.roll(e, 1, 0)
            s4 = s2 + pltpu.roll(s2, 2, 0)
            s8 = s4 + pltpu.roll(s4, 4, 0)
            s16 = s8 + pltpu.roll(s8, 8, 0)
            mixed = []
            for gi, (win, s) in enumerate(zip(POOL_WINDOWS, (s2, s4, s8, s16))):
                lo = gi * POOL_GROUP_DIM
                cnt = jnp.minimum(tok1, win).astype(_F32) if is_meta else float(win)
                pooled = s[POOL_HALO:, lo:lo + POOL_GROUP_DIM] / cnt
                mixed.append((pooled - u[:, lo:lo + POOL_GROUP_DIM]).astype(_BF16))
            return mixed

        def pool_proj(mixed):
            mixed = [_dot(mx, wgrp_ref[gi]) for gi, mx in enumerate(mixed)]
            mixed = (jnp.concatenate(mixed, axis=-1) * pscale_ref[...]).astype(_BF16)
            return _dot(mixed, wpp_ref[...])

        def gate_chunk(c):
            return _sigmoid(_dot(hn, win_ref[:, C_GP + c * GATE_CHUNK:C_GP + (c + 1) * GATE_CHUNK]))

        n_gate_chunks = (C_END - C_GP) // GATE_CHUNK
        side = {}
        side_work = [lambda: side.__setitem__("mixed", pool_sums()),
                     lambda: side.__setitem__("y_pool", pool_proj(side["mixed"]))]
        side_work += [functools.partial(lambda c: side.__setitem__(("gate", c), gate_chunk(c)), c)
                      for c in range(n_gate_chunks)]

        q_idx = lax.broadcasted_iota(jnp.int32, (1, BLOCK), 1)
        if is_meta:
            key_idx = lax.broadcasted_iota(jnp.int32, (BLOCK, 1), 0)
            masks = [((key_idx <= q_idx) & (key_idx < N_META), None)]
        else:
            key_idx = lax.broadcasted_iota(jnp.int32, (N_KEYS, 1), 0)
            first_block_ok = key_idx >= jnp.where(i > 0, 0, BLOCK)
            masks = [(first_block_ok, None)] + [(None, None)] * (n_blocks - 1)
        zeros_q = jnp.zeros((HEAD_DIM, GQA_GROUP * BLOCK), _BF16)

        def scores(b, kh):
            r0 = b * BLOCK
            lanes = slice((kh // 2) * LANES, (kh // 2 + 1) * LANES)
            if is_meta:
                kk = k[:, lanes]
            else:
                k_prv = k_carry if b == 0 else k[r0 - BLOCK:r0, :]
                kk = jnp.concatenate([k_prv[:, lanes], k[r0:r0 + BLOCK, lanes], k_meta[:, lanes]], axis=0)
            q_stack = jnp.concatenate(
                [q_t[(kh * GQA_GROUP + g) * HEAD_DIM:(kh * GQA_GROUP + g + 1) * HEAD_DIM, r0:r0 + BLOCK]
                 for g in range(GQA_GROUP)], axis=1)
            q_z = jnp.concatenate([q_stack, zeros_q] if kh % 2 == 0 else [zeros_q, q_stack], axis=0)
            return _dot(kk, q_z)

        def attend(b, kh, s_t):
            r0 = b * BLOCK
            ok_b, _ = masks[b]
            rows_v = slice(kh * HEAD_DIM, (kh + 1) * HEAD_DIM)
            if is_meta:
                vv_t = v_t[rows_v, :]
            else:
                v_prv_t = v_carry_t if b == 0 else v_t[:, r0 - BLOCK:r0]
                vv_t = jnp.concatenate(
                    [v_prv_t[rows_v, :], v_t[rows_v, r0:r0 + BLOCK], v_meta_t[rows_v, :]], axis=1)
            probs, inv_l = [], []
            for g in range(GQA_GROUP):
                head = kh * GQA_GROUP + g
                logits = s_t[:, g * BLOCK:(g + 1) * BLOCK]
                if not is_meta:
                    logits = logits + bias_ref[head]
                if ok_b is not None:
                    logits = jnp.where(ok_b, logits, NEG_INF)
                sink = sinks_ref[layer, head] * LOG2_E
                m = jnp.maximum(jnp.max(logits, axis=0, keepdims=True), sink)
                p = jnp.exp2(logits - m)
                denom = jnp.sum(p, axis=0, keepdims=True) + jnp.exp2(sink - m)
                probs.append(p.astype(_BF16))
                inv_l.append(1.0 / denom)
            return vv_t, jnp.concatenate(probs, axis=1), inv_l

        def weighted_values(vv_t, p_t, inv_l):
            o_t = _dot(vv_t, p_t)
            return [o_t[:, g * BLOCK:(g + 1) * BLOCK] * inv_l[g] for g in range(GQA_GROUP)]

        head_rows = {b: [] for b in range(n_blocks)}

        def post_work(b_lo, b_hi):
            r0, r1 = b_lo * BLOCK, b_hi * BLOCK
            st = {}

            def y_piece(c):
                if "attn_t" not in st:
                    st["attn_t"] = jnp.concatenate(
                        [jnp.concatenate(head_rows[b], axis=0) for b in range(b_lo, b_hi)], axis=1).astype(_BF16)
                st["y", c] = _dot_tn(st["attn_t"], wap_ref[:, c * POST_CHUNK:(c + 1) * POST_CHUNK])

            def z_piece(c):
                if "merged" not in st:
                    y_attn = jnp.concatenate([st["y", cc] for cc in range(n_post_chunks)], axis=-1)
                    g_pool = jnp.concatenate([side["gate", cc][r0:r1] for cc in range(n_gate_chunks // 2)], axis=-1)
                    g_attn = jnp.concatenate(
                        [side["gate", cc][r0:r1] for cc in range(n_gate_chunks // 2, n_gate_chunks)], axis=-1)
                    st["merged"] = (g_pool * side["y_pool"][r0:r1] + g_attn * y_attn).astype(_BF16)
                st["z", c] = _dot(st["merged"], wo_ref[:, c * POST_CHUNK:(c + 1) * POST_CHUNK])

            def finish():
                z = jnp.concatenate([st["z", cc] for cc in range(n_post_chunks)], axis=-1)
                store(r0, r1, h[r0:r1] + _rmsnorm(z, g_ref[MIX_POST:MIX_POST + 1, :]))

            n_post_chunks = D_MODEL // POST_CHUNK
            return ([functools.partial(y_piece, c) for c in range(n_post_chunks)]
                    + [functools.partial(z_piece, c) for c in range(n_post_chunks)] + [finish])

        n_groups = 2 if n_blocks % 2 == 0 else 1
        blocks_per_group = n_blocks // n_groups
        steps = [(b, kh) for b in range(n_blocks) for kh in range(N_KV_HEADS)]
        steps_per_group = blocks_per_group * N_KV_HEADS
        queue = list(side_work)
        queue.pop(0)()
        s_next = scores(*steps[0])
        v_t = _proj_t(win_ref[:, C_V:C_GP], hn).astype(_BF16)
        if is_meta:
            vmeta_ref[...] = v_t[:, :N_META]
        else:
            v_meta_t = vmeta_ref[...]
            v_carry_t = vprev_ref[rd]
            vprev_ref[wr] = v_t[:, rows - BLOCK:]
        pending = None
        for n, (b, kh) in enumerate(steps):
            s_cur = s_next
            if n + 1 < len(steps):
                s_next = scores(*steps[n + 1])
            steps_left = steps_per_group - n % steps_per_group
            for _ in range(-(-len(queue) // steps_left)):
                queue.pop(0)()
            if pending is not None:
                head_rows[pending[0]] += weighted_values(*pending[1])
            soft = attend(b, kh, s_cur)
            pending = (b, soft) if steps_left > 1 else None
            if pending is None:
                head_rows[b] += weighted_values(*soft)
            if steps_left == 1:
                grp = n // steps_per_group
                queue += post_work(grp * blocks_per_group, (grp + 1) * blocks_per_group)
        while queue:
            queue.pop(0)()

    def store_meta(r0, r1, value):
        om_ref[...] = value[:N_META, :]

    def store_rows(r0, r1, value):
        o_ref[r0:r1, :] = value

    @pl.when(i == 0)
    def _():
        kprev_ref[0] = jnp.zeros((BLOCK, KV_DIM), _BF16)
        vprev_ref[0] = jnp.zeros((KV_DIM, BLOCK), _BF16)
        key_idx = lax.broadcasted_iota(jnp.int32, (N_KEYS, 1), 0)
        q_idx = lax.broadcasted_iota(jnp.int32, (1, BLOCK), 1)
        in_blk = jnp.bitwise_and(key_idx, BLOCK - 1)
        is_meta_key = key_idx >= 2 * BLOCK
        dist = jnp.where(key_idx < BLOCK, q_idx + BLOCK - in_blk, q_idx - in_blk)
        ok = ((dist >= 0) & (dist < WINDOW)) | is_meta_key
        alibi_d = jnp.where(is_meta_key, 0, dist).astype(_F32)
        for head in range(N_HEADS):
            bias_ref[head] = jnp.where(ok, (-_alibi_slope(head) * LOG2_E) * alibi_d, NEG_INF)
        hm = jnp.concatenate([hm_ref[...], jnp.zeros((BLOCK - N_META, D_MODEL), _F32)], axis=0)
        mixer_rows(hm, True, store_meta)

    mixer_rows(h_ref[...], False, store_rows)


def _mixer_call(h, hm, norm_g, w_in, w_grp, pscale, w_pp, sinks, w_ap, w_o, layer, cast_next):
    n_steps = h.shape[0] // MIXER_TILE_ROWS
    row_spec = pl.BlockSpec((MIXER_TILE_ROWS, D_MODEL), lambda i: (i, 0))
    plans = [_cast_plan(w, index, n_steps) for w, index in cast_next]
    return pl.pallas_call(
        functools.partial(_mixer_kernel, layer, len(plans)),
        grid=(n_steps,),
        in_specs=[
            row_spec,
            _const_spec((N_META, D_MODEL), (0, 0)),
            _const_spec((None,) + norm_g.shape[1:], (layer, 0, 0)),
            _const_spec(w_in.shape, (0, 0)),
            _const_spec(w_grp.shape, (0, 0, 0)),
            _const_spec((None,) + pscale.shape[1:], (layer, 0, 0)),
            _const_spec(w_pp.shape, (0, 0)),
            pl.BlockSpec(memory_space=pltpu.SMEM),
            _const_spec(w_ap.shape, (0, 0)),
            _const_spec(w_o.shape, (0, 0)),
        ] + [p[0] for p in plans],
        out_specs=[row_spec, pl.BlockSpec((N_META, D_MODEL), lambda i: (0, 0))] + [p[1] for p in plans],
        out_shape=[jax.ShapeDtypeStruct(h.shape, h.dtype), jax.ShapeDtypeStruct(hm.shape, hm.dtype)]
        + [p[2] for p in plans],
        scratch_shapes=[
            pltpu.VMEM((2, POOL_HALO, POOL_DIM), _F32),
            pltpu.VMEM((2, BLOCK, KV_DIM), _BF16),
            pltpu.VMEM((2, KV_DIM, BLOCK), _BF16),
            pltpu.VMEM((N_META, KV_DIM), _BF16),
            pltpu.VMEM((KV_DIM, N_META), _BF16),
            pltpu.VMEM((N_HEADS, N_KEYS, BLOCK), _F32),
        ],
        compiler_params=pltpu.CompilerParams(
            dimension_semantics=("arbitrary",), vmem_limit_bytes=VMEM_LIMIT),
        name="gated_mixers",
    )(h, hm, norm_g, w_in, w_grp, pscale, w_pp, sinks, w_ap, w_o, *[w for w, _ in cast_next])


def kernel(x, meta_tokens, norm_g, ffn_w_in, ffn_w_out, w_in, pool_w_grp, pool_scale, w_pool_proj, sinks,
           w_attn_proj, w_o):
    batch, seq, d = x.shape
    assert batch == 1 and d == D_MODEL and seq % FFN_TILE_ROWS == 0 and seq % MIXER_TILE_ROWS == 0
    assert meta_tokens.shape == (N_META, D_MODEL) and POOL_HALO == N_META
    depth = norm_g.shape[0]
    n_groups = len(POOL_WINDOWS)
    w_grp_v = pool_w_grp.reshape(depth, n_groups * POOL_GROUP_DIM, POOL_GROUP_DIM)
    pscale = pool_scale.reshape(depth, 1, POOL_DIM)

    def ffn_weights(l, j):
        return [(ffn_w_in, (l, j)), (ffn_w_out, (l, j))]

    def mixer_weights(l):
        return [(w_in, (l,)), (w_grp_v, (l,)), (w_pool_proj, (l,)), (w_attn_proj, (l,)), (w_o, (l,))]

    h, hm = x[0], meta_tokens.astype(x.dtype)
    w_ffn1 = _cast_call(ffn_weights(0, 0), FIRST_CAST_STEPS)
    w_mix = None
    for l in range(depth):
        last = l + 1 == depth
        h, hm, *cast = _ffn_call(h, hm, norm_g, w_ffn1[0], w_ffn1[1], l, FFN1_PRE, FFN1_POST,
                                 mixer_weights(l) if w_mix is None else [])
        w_mix = cast or w_mix
        w_grp = w_mix[1].reshape(n_groups, POOL_GROUP_DIM, POOL_GROUP_DIM)
        ahead = ffn_weights(l, 1) + ([] if last else ffn_weights(l + 1, 0) + mixer_weights(l + 1))
        h, hm, *cast = _mixer_call(h, hm, norm_g, w_mix[0], w_grp, pscale, w_mix[2], sinks, w_mix[3], w_mix[4],
                                   l, ahead)
        h, hm = _ffn_call(h, hm, norm_g, cast[0], cast[1], l, FFN2_PRE, FFN2_POST, [])
        w_ffn1, w_mix = cast[2:4], cast[4:]
    return h[None]
```

```python
import functools

import jax
import jax.numpy as jnp
from jax import lax
from jax.experimental import pallas as pl
from jax.experimental.pallas import tpu as pltpu

D_MODEL = 1024
N_META = 16
N_HEADS = 16
N_KV_HEADS = 4
HEAD_DIM = 64
GQA_GROUP = N_HEADS // N_KV_HEADS
WINDOW = 128
LANES = 128
BLOCK = 128
POOL_WINDOWS = (2, 4, 8, 16)
POOL_GROUP_DIM = 128
POOL_DIM = 512
KV_DIM = 256
N_KEYS = 2 * BLOCK + N_META
D_FF = 2816
RMS_EPS = 1e-6
NEG_INF = -1e30
LOG2_E = 1.4426950408889634
FFN1_PRE, FFN1_POST, MIX_PRE, MIX_POST, FFN2_PRE, FFN2_POST = range(6)

C_U, C_Q, C_K, C_V, C_GP, C_GA, C_END = 0, 512, 1536, 1792, 2048, 3072, 4096

GATE_CHUNK = 256
POOL_HALO = 16
MIXER_TILE_ROWS = 512
POST_CHUNK = 256
FFN_TILE_ROWS = 1024
FFN_SUB_ROWS = 256
VMEM_LIMIT = 56 * 1024 * 1024
FIRST_CAST_STEPS = 16
BF16_SUBLANES = 16

_BF16 = jnp.bfloat16
_F32 = jnp.float32


def _rmsnorm(x, g):
    ms = jnp.mean(x * x, axis=-1, keepdims=True)
    return x * lax.rsqrt(ms + RMS_EPS) * g


def _sigmoid(x):
    return 0.5 * jnp.tanh(0.5 * x) + 0.5


def _dot(a, b):
    return jnp.dot(a, b, preferred_element_type=_F32)


def _dot_tn(a, b):
    return lax.dot_general(a, b, (((0,), (0,)), ((), ())), preferred_element_type=_F32)


def _proj_t(w, x):
    return lax.dot_general(w, x, (((0,), (1,)), ((), ())), preferred_element_type=_F32)


def _const_spec(block_shape, index):
    return pl.BlockSpec(block_shape, lambda i: index, pipeline_mode=pl.Buffered(1))


def _cast_plan(w, index, n_steps):
    rows, cols = w.shape[len(index):]
    n_blocks = max(d for d in range(1, n_steps + 1) if n_steps % d == 0 and rows % (d * BF16_SUBLANES) == 0)
    block_rows, revisit = rows // n_blocks, n_steps // n_blocks
    in_spec = pl.BlockSpec((None,) * len(index) + (block_rows, cols), lambda i: tuple(index) + (i // revisit, 0))
    out_spec = pl.BlockSpec((block_rows, cols), lambda i: (i // revisit, 0))
    return in_spec, out_spec, jax.ShapeDtypeStruct((rows, cols), _BF16)


def _cast_blocks(src_refs, dst_refs):
    for src, dst in zip(src_refs, dst_refs):
        dst[...] = src[...].astype(dst.dtype)


def _cast_kernel(*refs):
    _cast_blocks(refs[:len(refs) // 2], refs[len(refs) // 2:])


def _cast_call(weights, n_steps):
    plans = [_cast_plan(w, index, n_steps) for w, index in weights]
    return pl.pallas_call(
        _cast_kernel,
        grid=(n_steps,),
        in_specs=[p[0] for p in plans],
        out_specs=[p[1] for p in plans],
        out_shape=[p[2] for p in plans],
        compiler_params=pltpu.CompilerParams(dimension_semantics=("arbitrary",), vmem_limit_bytes=VMEM_LIMIT),
        name="cast_bf16",
    )(*[w for w, _ in weights])


def _ffn_up(h, g_ref, win_ref, pre):
    xn = _rmsnorm(h, g_ref[pre:pre + 1, :]).astype(_BF16)
    return _dot(xn, win_ref[:, :D_FF]), _dot(xn, win_ref[:, D_FF:])


def _ffn_down(h, gate, up, g_ref, wout_ref, post):
    act = (gate * _sigmoid(gate) * up).astype(_BF16)
    y = _dot(act, wout_ref[...])
    return h + 0.5 * _rmsnorm(y, g_ref[post:post + 1, :])


def _ffn_rows(h, g_ref, win_ref, wout_ref, pre, post):
    gate, up = _ffn_up(h, g_ref, win_ref, pre)
    return _ffn_down(h, gate, up, g_ref, wout_ref, post)


def _ffn_kernel(pre, post, n_cast, *refs):
    h_ref, hm_ref, g_ref, win_ref, wout_ref = refs[:5]
    o_ref, om_ref = refs[5 + n_cast:7 + n_cast]
    _cast_blocks(refs[5:5 + n_cast], refs[7 + n_cast:])

    @pl.when(pl.program_id(0) == 0)
    def _():
        om_ref[...] = _ffn_rows(hm_ref[...], g_ref, win_ref, wout_ref, pre, post)

    starts = list(range(0, FFN_TILE_ROWS, FFN_SUB_ROWS))
    nxt = _ffn_up(h_ref[0:FFN_SUB_ROWS, :], g_ref, win_ref, pre)
    for j, r0 in enumerate(starts):
        gate, up = nxt
        if j + 1 < len(starts):
            nxt = _ffn_up(h_ref[starts[j + 1]:starts[j + 1] + FFN_SUB_ROWS, :], g_ref, win_ref, pre)
        o_ref[r0:r0 + FFN_SUB_ROWS, :] = _ffn_down(
            h_ref[r0:r0 + FFN_SUB_ROWS, :], gate, up, g_ref, wout_ref, post)


def _ffn_call(h, hm, norm_g, w_in, w_out, layer, pre, post, cast_next):
    n_steps = h.shape[0] // FFN_TILE_ROWS
    row_spec = pl.BlockSpec((FFN_TILE_ROWS, D_MODEL), lambda i: (i, 0))
    plans = [_cast_plan(w, index, n_steps) for w, index in cast_next]
    return pl.pallas_call(
        functools.partial(_ffn_kernel, pre, post, len(plans)),
        grid=(n_steps,),
        in_specs=[
            row_spec,
            _const_spec((N_META, D_MODEL), (0, 0)),
            _const_spec((None,) + norm_g.shape[1:], (layer, 0, 0)),
            _const_spec(w_in.shape, (0, 0)),
            _const_spec(w_out.shape, (0, 0)),
        ] + [p[0] for p in plans],
        out_specs=[row_spec, pl.BlockSpec((N_META, D_MODEL), lambda i: (0, 0))] + [p[1] for p in plans],
        out_shape=[jax.ShapeDtypeStruct(h.shape, h.dtype), jax.ShapeDtypeStruct(hm.shape, hm.dtype)]
        + [p[2] for p in plans],
        compiler_params=pltpu.CompilerParams(
            dimension_semantics=("arbitrary",), vmem_limit_bytes=VMEM_LIMIT),
        name="ffn_half_step",
    )(h, hm, norm_g, w_in, w_out, *[w for w, _ in cast_next])


def _alibi_slope(head):
    return 2.0 ** (-8.0 * (head + 1) / N_HEADS)


def _mixer_kernel(layer, n_cast, *refs):
    h_ref, hm_ref, g_ref, win_ref, wgrp_ref, pscale_ref, wpp_ref, sinks_ref, wap_ref, wo_ref = refs[:10]
    o_ref, om_ref = refs[10 + n_cast:12 + n_cast]
    ucarry_ref, kprev_ref, vprev_ref, kmeta_ref, vmeta_ref, bias_ref = refs[12 + 2 * n_cast:]
    _cast_blocks(refs[10:10 + n_cast], refs[12 + n_cast:12 + 2 * n_cast])
    i = pl.program_id(0)
    rd = lax.rem(i, 2)
    wr = 1 - rd

    def mixer_rows(h, is_meta, store):
        rows = h.shape[0]
        n_blocks = rows // BLOCK
        hn = _rmsnorm(h, g_ref[MIX_PRE:MIX_PRE + 1, :]).astype(_BF16)
        row_idx = lax.broadcasted_iota(jnp.int32, (rows, 1), 0)

        k = _dot(hn, win_ref[:, C_K:C_V]).astype(_BF16)
        q_t = (_proj_t(win_ref[:, C_Q:C_K], hn) * (HEAD_DIM ** -0.5 * LOG2_E)).astype(_BF16)

        if is_meta:
            kmeta_ref[...] = k[:N_META, :]
        else:
            k_meta = kmeta_ref[...]
            k_carry = kprev_ref[rd]
            kprev_ref[wr] = k[rows - BLOCK:, :]

        def pool_sums():
            u = _dot(hn, win_ref[:, C_U:C_Q])
            if is_meta:
                u = jnp.where(row_idx < N_META, u, 0.0)
                history = jnp.zeros((POOL_HALO, POOL_DIM), _F32)
                ucarry_ref[0] = u[:POOL_HALO, :]
                tok1 = jnp.minimum(row_idx + 1, POOL_WINDOWS[-1])
            else:
                history = ucarry_ref[rd]
                ucarry_ref[wr] = u[rows - POOL_HALO:, :]
            e = jnp.concatenate([history, u], axis=0)
            s2 = e + pltpu.roll(e, 1, 0)
            s4 = s2 + pltpu.roll(s2, 2, 0)
            s8 = s4 + pltpu.roll(s4, 4, 0)
            s16 = s8 + pltpu.roll(s8, 8, 0)
            mixed = []
            for gi, (win, s) in enumerate(zip(POOL_WINDOWS, (s2, s4, s8, s16))):
                lo = gi * POOL_GROUP_DIM
                cnt = jnp.minimum(tok1, win).astype(_F32) if is_meta else float(win)
                pooled = s[POOL_HALO:, lo:lo + POOL_GROUP_DIM] / cnt
                mixed.append((pooled - u[:, lo:lo + POOL_GROUP_DIM]).astype(_BF16))
            return mixed

        def pool_proj(mixed):
            mixed = [_dot(mx, wgrp_ref[gi]) for gi, mx in enumerate(mixed)]
            mixed = (jnp.concatenate(mixed, axis=-1) * pscale_ref[...]).astype(_BF16)
            return _dot(mixed, wpp_ref[...])

        def gate_chunk(c):
            return _sigmoid(_dot(hn, win_ref[:, C_GP + c * GATE_CHUNK:C_GP + (c + 1) * GATE_CHUNK]))

        n_gate_chunks = (C_END - C_GP) // GATE_CHUNK
        side = {}
        side_work = [lambda: side.__setitem__("mixed", pool_sums()),
                     lambda: side.__setitem__("y_pool", pool_proj(side["mixed"]))]
        side_work += [functools.partial(lambda c: side.__setitem__(("gate", c), gate_chunk(c)), c)
                      for c in range(n_gate_chunks)]

        q_idx = lax.broadcasted_iota(jnp.int32, (1, BLOCK), 1)
        if is_meta:
            key_idx = lax.broadcasted_iota(jnp.int32, (BLOCK, 1), 0)
            masks = [((key_idx <= q_idx) & (key_idx < N_META), None)]
        else:
            key_idx = lax.broadcasted_iota(jnp.int32, (N_KEYS, 1), 0)
            first_block_ok = key_idx >= jnp.where(i > 0, 0, BLOCK)
            masks = [(first_block_ok, None)] + [(None, None)] * (n_blocks - 1)
        zeros_q = jnp.zeros((HEAD_DIM, GQA_GROUP * BLOCK), _BF16)

        def scores(b, kh):
            r0 = b * BLOCK
            lanes = slice((kh // 2) * LANES, (kh // 2 + 1) * LANES)
            if is_meta:
                kk = k[:, lanes]
            else:
                k_prv = k_carry if b == 0 else k[r0 - BLOCK:r0, :]
                kk = jnp.concatenate([k_prv[:, lanes], k[r0:r0 + BLOCK, lanes], k_meta[:, lanes]], axis=0)
            q_stack = jnp.concatenate(
                [q_t[(kh * GQA_GROUP + g) * HEAD_DIM:(kh * GQA_GROUP + g + 1) * HEAD_DIM, r0:r0 + BLOCK]
                 for g in range(GQA_GROUP)], axis=1)
            q_z = jnp.concatenate([q_stack, zeros_q] if kh % 2 == 0 else [zeros_q, q_stack], axis=0)
            return _dot(kk, q_z)

        def attend(b, kh, s_t):
            r0 = b * BLOCK
            ok_b, _ = masks[b]
            rows_v = slice(kh * HEAD_DIM, (kh + 1) * HEAD_DIM)
            if is_meta:
                vv_t = v_t[rows_v, :]
            else:
                v_prv_t = v_carry_t if b == 0 else v_t[:, r0 - BLOCK:r0]
                vv_t = jnp.concatenate(
                    [v_prv_t[rows_v, :], v_t[rows_v, r0:r0 + BLOCK], v_meta_t[rows_v, :]], axis=1)
            probs, inv_l = [], []
            for g in range(GQA_GROUP):
                head = kh * GQA_GROUP + g
                logits = s_t[:, g * BLOCK:(g + 1) * BLOCK]
                if not is_meta:
                    logits = logits + bias_ref[head]
                if ok_b is not None:
                    logits = jnp.where(ok_b, logits, NEG_INF)
                sink = sinks_ref[layer, head] * LOG2_E
                m = jnp.maximum(jnp.max(logits, axis=0, keepdims=True), sink)
                p = jnp.exp2(logits - m)
                denom = jnp.sum(p, axis=0, keepdims=True) + jnp.exp2(sink - m)
                probs.append(p.astype(_BF16))
                inv_l.append(1.0 / denom)
            return vv_t, jnp.concatenate(probs, axis=1), inv_l

        def weighted_values(vv_t, p_t, inv_l):
            o_t = _dot(vv_t, p_t)
            return [o_t[:, g * BLOCK:(g + 1) * BLOCK] * inv_l[g] for g in range(GQA_GROUP)]

        head_rows = {b: [] for b in range(n_blocks)}

        def post_work(b_lo, b_hi):
            r0, r1 = b_lo * BLOCK, b_hi * BLOCK
            st = {}

            def y_piece(c):
                if "attn_t" not in st:
                    st["attn_t"] = jnp.concatenate(
                        [jnp.concatenate(head_rows[b], axis=0) for b in range(b_lo, b_hi)], axis=1).astype(_BF16)
                st["y", c] = _dot_tn(st["attn_t"], wap_ref[:, c * POST_CHUNK:(c + 1) * POST_CHUNK])

            def z_piece(c):
                if "merged" not in st:
                    y_attn = jnp.concatenate([st["y", cc] for cc in range(n_post_chunks)], axis=-1)
                    g_pool = jnp.concatenate([side["gate", cc][r0:r1] for cc in range(n_gate_chunks // 2)], axis=-1)
                    g_attn = jnp.concatenate(
                        [side["gate", cc][r0:r1] for cc in range(n_gate_chunks // 2, n_gate_chunks)], axis=-1)
                    st["merged"] = (g_pool * side["y_pool"][r0:r1] + g_attn * y_attn).astype(_BF16)
                st["z", c] = _dot(st["merged"], wo_ref[:, c * POST_CHUNK:(c + 1) * POST_CHUNK])

            def finish():
                z = jnp.concatenate([st["z", cc] for cc in range(n_post_chunks)], axis=-1)
                store(r0, r1, h[r0:r1] + _rmsnorm(z, g_ref[MIX_POST:MIX_POST + 1, :]))

            n_post_chunks = D_MODEL // POST_CHUNK
            return ([functools.partial(y_piece, c) for c in range(n_post_chunks)]
                    + [functools.partial(z_piece, c) for c in range(n_post_chunks)] + [finish])

        n_groups = 2 if n_blocks % 2 == 0 else 1
        blocks_per_group = n_blocks // n_groups
        steps = [(b, kh) for b in range(n_blocks) for kh in range(N_KV_HEADS)]
        steps_per_group = blocks_per_group * N_KV_HEADS
        queue = list(side_work)
        queue.pop(0)()
        s_next = scores(*steps[0])
        v_t = _proj_t(win_ref[:, C_V:C_GP], hn).astype(_BF16)
        if is_meta:
            vmeta_ref[...] = v_t[:, :N_META]
        else:
            v_meta_t = vmeta_ref[...]
            v_carry_t = vprev_ref[rd]
            vprev_ref[wr] = v_t[:, rows - BLOCK:]
        pending = None
        for n, (b, kh) in enumerate(steps):
            s_cur = s_next
            if n + 1 < len(steps):
                s_next = scores(*steps[n + 1])
            steps_left = steps_per_group - n % steps_per_group
            for _ in range(-(-len(queue) // steps_left)):
                queue.pop(0)()
            if pending is not None:
                head_rows[pending[0]] += weighted_values(*pending[1])
            soft = attend(b, kh, s_cur)
            pending = (b, soft) if steps_left > 1 else None
            if pending is None:
                head_rows[b] += weighted_values(*soft)
            if steps_left == 1:
                grp = n // steps_per_group
                queue += post_work(grp * blocks_per_group, (grp + 1) * blocks_per_group)
        while queue:
            queue.pop(0)()

    def store_meta(r0, r1, value):
        om_ref[...] = value[:N_META, :]

    def store_rows(r0, r1, value):
        o_ref[r0:r1, :] = value

    @pl.when(i == 0)
    def _():
        kprev_ref[0] = jnp.zeros((BLOCK, KV_DIM), _BF16)
        vprev_ref[0] = jnp.zeros((KV_DIM, BLOCK), _BF16)
        key_idx = lax.broadcasted_iota(jnp.int32, (N_KEYS, 1), 0)
        q_idx = lax.broadcasted_iota(jnp.int32, (1, BLOCK), 1)
        in_blk = jnp.bitwise_and(key_idx, BLOCK - 1)
        is_meta_key = key_idx >= 2 * BLOCK
        dist = jnp.where(key_idx < BLOCK, q_idx + BLOCK - in_blk, q_idx - in_blk)
        ok = ((dist >= 0) & (dist < WINDOW)) | is_meta_key
        alibi_d = jnp.where(is_meta_key, 0, dist).astype(_F32)
        for head in range(N_HEADS):
            bias_ref[head] = jnp.where(ok, (-_alibi_slope(head) * LOG2_E) * alibi_d, NEG_INF)
        hm = jnp.concatenate([hm_ref[...], jnp.zeros((BLOCK - N_META, D_MODEL), _F32)], axis=0)
        mixer_rows(hm, True, store_meta)

    mixer_rows(h_ref[...], False, store_rows)


def _mixer_call(h, hm, norm_g, w_in, w_grp, pscale, w_pp, sinks, w_ap, w_o, layer, cast_next):
    n_steps = h.shape[0] // MIXER_TILE_ROWS
    row_spec = pl.BlockSpec((MIXER_TILE_ROWS, D_MODEL), lambda i: (i, 0))
    plans = [_cast_plan(w, index, n_steps) for w, index in cast_next]
    return pl.pallas_call(
        functools.partial(_mixer_kernel, layer, len(plans)),
        grid=(n_steps,),
        in_specs=[
            row_spec,
            _const_spec((N_META, D_MODEL), (0, 0)),
            _const_spec((None,) + norm_g.shape[1:], (layer, 0, 0)),
            _const_spec(w_in.shape, (0, 0)),
            _const_spec(w_grp.shape, (0, 0, 0)),
            _const_spec((None,) + pscale.shape[1:], (layer, 0, 0)),
            _const_spec(w_pp.shape, (0, 0)),
            pl.BlockSpec(memory_space=pltpu.SMEM),
            _const_spec(w_ap.shape, (0, 0)),
            _const_spec(w_o.shape, (0, 0)),
        ] + [p[0] for p in plans],
        out_specs=[row_spec, pl.BlockSpec((N_META, D_MODEL), lambda i: (0, 0))] + [p[1] for p in plans],
        out_shape=[jax.ShapeDtypeStruct(h.shape, h.dtype), jax.ShapeDtypeStruct(hm.shape, hm.dtype)]
        + [p[2] for p in plans],
        scratch_shapes=[
            pltpu.VMEM((2, POOL_HALO, POOL_DIM), _F32),
            pltpu.VMEM((2, BLOCK, KV_DIM), _BF16),
            pltpu.VMEM((2, KV_DIM, BLOCK), _BF16),
            pltpu.VMEM((N_META, KV_DIM), _BF16),
            pltpu.VMEM((KV_DIM, N_META), _BF16),
            pltpu.VMEM((N_HEADS, N_KEYS, BLOCK), _F32),
        ],
        compiler_params=pltpu.CompilerParams(
            dimension_semantics=("arbitrary",), vmem_limit_bytes=VMEM_LIMIT),
        name="gated_mixers",
    )(h, hm, norm_g, w_in, w_grp, pscale, w_pp, sinks, w_ap, w_o, *[w for w, _ in cast_next])


def kernel(x, meta_tokens, norm_g, ffn_w_in, ffn_w_out, w_in, pool_w_grp, pool_scale, w_pool_proj, sinks,
           w_attn_proj, w_o):
    batch, seq, d = x.shape
    assert batch == 1 and d == D_MODEL and seq % FFN_TILE_ROWS == 0 and seq % MIXER_TILE_ROWS == 0
    assert meta_tokens.shape == (N_META, D_MODEL) and POOL_HALO == N_META
    depth = norm_g.shape[0]
    n_groups = len(POOL_WINDOWS)
    w_grp_v = pool_w_grp.reshape(depth, n_groups * POOL_GROUP_DIM, POOL_GROUP_DIM)
    pscale = pool_scale.reshape(depth, 1, POOL_DIM)

    def ffn_weights(l, j):
        return [(ffn_w_in, (l, j)), (ffn_w_out, (l, j))]

    def mixer_weights(l):
        return [(w_in, (l,)), (w_grp_v, (l,)), (w_pool_proj, (l,)), (w_attn_proj, (l,)), (w_o, (l,))]

    h, hm = x[0], meta_tokens.astype(x.dtype)
    w_ffn1 = _cast_call(ffn_weights(0, 0), FIRST_CAST_STEPS)
    w_mix = None
    for l in range(depth):
        last = l + 1 == depth
        h, hm, *cast = _ffn_call(h, hm, norm_g, w_ffn1[0], w_ffn1[1], l, FFN1_PRE, FFN1_POST,
                                 mixer_weights(l) if w_mix is None else [])
        w_mix = cast or w_mix
        w_grp = w_mix[1].reshape(n_groups, POOL_GROUP_DIM, POOL_GROUP_DIM)
        ahead = ffn_weights(l, 1) + ([] if last else ffn_weights(l + 1, 0) + mixer_weights(l + 1))
        h, hm, *cast = _mixer_call(h, hm, norm_g, w_mix[0], w_grp, pscale, w_mix[2], sinks, w_mix[3], w_mix[4],
                                   l, ahead)
        h, hm = _ffn_call(h, hm, norm_g, cast[0], cast[1], l, FFN2_PRE, FFN2_POST, [])
        w_ffn1, w_mix = cast[2:4], cast[4:]
    return h[None]
```
